```python
import math
import jax, jax.numpy as jnp
from jax import lax
import numpy as np


D_MODEL = 2048
BATCH = 4
SEQ = 8192
DEPTH = 4

GDN_HEADS = D_MODEL // 256
GDN_HEAD_DIM = 128
GDN_WIDTH = GDN_HEADS * GDN_HEAD_DIM
CONV_WIDTH = 4
CHUNK = 64
DSA_HEADS = D_MODEL // 256
DSA_HEAD_DIM = 128
DSA_WIDTH = DSA_HEADS * DSA_HEAD_DIM
IDX_HEADS = D_MODEL // 256
IDX_HEAD_DIM = 64
TOPK_MAX = 256
Q_BLOCK = 128
REL_BUCKETS = 32
REL_MAX_DIST = 128
D_FF = -(-8 * D_MODEL // (3 * 256)) * 256
DEEPNORM_ALPHA = (2 * DEPTH) ** 0.25
DEEPNORM_BETA = (8 * DEPTH) ** -0.25
LN_EPS = 1e-5
RMS_EPS = 1e-6

SPLIT_SIZES = (
    3 * GDN_WIDTH,
    GDN_HEADS,
    GDN_HEADS,
    GDN_WIDTH,
    DSA_WIDTH, DSA_WIDTH, DSA_WIDTH,
    IDX_HEADS * IDX_HEAD_DIM,
    IDX_HEAD_DIM,
    IDX_HEADS,
    D_MODEL,
    D_MODEL,
)
D_IN = sum(SPLIT_SIZES)

kernel_name = 'hybrid_gdn_dsa_deepnorm_block'


def layer_norm(x, g, b):
    xf = x.astype(jnp.float32)
    mu = jnp.mean(xf, axis=-1, keepdims=True)
    var = jnp.mean(jnp.square(xf - mu), axis=-1, keepdims=True)
    return ((xf - mu) * lax.rsqrt(var + LN_EPS) * g.astype(jnp.float32) + b.astype(jnp.float32)).astype(x.dtype)


def l2_normalize(x):
    return x * lax.rsqrt(jnp.sum(jnp.square(x), axis=-1, keepdims=True) + RMS_EPS)


def causal_short_conv(x, w):
    S = x.shape[1]
    K = w.shape[0]
    xp = jnp.pad(x, ((0, 0), (K - 1, 0), (0, 0)))
    y = sum(xp[:, j:j + S] * w[j] for j in range(K))
    return jax.nn.silu(y)


def gated_delta_rule_chunked(q, k, v, g, beta):
    B, S, H, Dk = q.shape
    Dv = v.shape[-1]
    N = S // CHUNK
    f32 = jnp.float32
    q = l2_normalize(q.astype(f32)) * (Dk ** -0.5)
    k = l2_normalize(k.astype(f32))

    def chunks(a):
        return jnp.moveaxis(a.reshape(B, N, CHUNK, H, *a.shape[3:]), 3, 1)

    q, k, v = chunks(q), chunks(k), chunks(v.astype(f32))
    g, beta = chunks(g.astype(f32)), chunks(beta.astype(f32))
    gc = jnp.cumsum(g, axis=-1)
    causal = jnp.tril(jnp.ones((CHUNK, CHUNK), dtype=bool))
    strict = jnp.tril(jnp.ones((CHUNK, CHUNK), dtype=bool), k=-1)
    decay = jnp.exp(jnp.where(causal, gc[..., :, None] - gc[..., None, :], -jnp.inf))
    k_beta = k * beta[..., None]
    v_beta = v * beta[..., None]
    m = jnp.where(strict, jnp.einsum('bhnid,bhnjd->bhnij', k_beta, k) * decay, 0.0)
    a = m + jnp.eye(CHUNK, dtype=f32)
    u = lax.linalg.triangular_solve(a, v_beta, left_side=True, lower=True, unit_diagonal=True)
    w = lax.linalg.triangular_solve(a, k_beta * jnp.exp(gc)[..., None], left_side=True, lower=True, unit_diagonal=True)
    intra = jnp.einsum('bhnid,bhnjd->bhnij', q, k) * decay
    q_dec = q * jnp.exp(gc)[..., None]
    k_dec = k * jnp.exp(gc[..., -1:] - gc)[..., None]
    g_tot = jnp.exp(gc[..., -1])

    def step(state, inp):
        q_n, k_n, u_n, w_n, a_n, gt = inp
        v_new = u_n - jnp.einsum('bhcd,bhde->bhce', w_n, state)
        o = jnp.einsum('bhcd,bhde->bhce', q_n, state) + jnp.einsum('bhij,bhje->bhie', a_n, v_new)
        state = state * gt[..., None, None] + jnp.einsum('bhcd,bhce->bhde', k_n, v_new)
        return state, o

    xs = tuple(jnp.moveaxis(t, 2, 0) for t in (q_dec, k_dec, u, w, intra, g_tot))
    state0 = jnp.zeros((B, H, Dk, Dv), f32)
    _, o = lax.scan(step, state0, xs)
    return jnp.transpose(o, (1, 0, 3, 2, 4)).reshape(B, S, H, Dv)


def t5_bucket(dist):
    n = jnp.maximum(dist, 0)
    max_exact = REL_BUCKETS // 2
    log_ratio = jnp.log(jnp.maximum(n, 1).astype(jnp.float32) / max_exact) / math.log(REL_MAX_DIST / max_exact)
    large = max_exact + (log_ratio * (REL_BUCKETS - max_exact)).astype(jnp.int32)
    large = jnp.minimum(large, REL_BUCKETS - 1)
    return jnp.where(n < max_exact, n, large)


def dsa_sparse_attention(q, k, v, q_idx, k_idx, w_idx, rel_bias):
    B, S, H, D = q.shape
    topk = min(TOPK_MAX, S // 4)
    nb = S // Q_BLOCK
    f32 = jnp.float32
    k_idx = k_idx.astype(f32)
    key_pos = jnp.arange(S)
    gather = jax.vmap(lambda src, ids: src[ids])

    def blocks(a):
        return jnp.moveaxis(a.reshape(B, nb, Q_BLOCK, *a.shape[2:]), 1, 0)

    def attend_block(inp):
        q_b, qi_b, wi_b, t0 = inp
        q_pos = t0 + jnp.arange(Q_BLOCK)
        rel = jax.nn.relu(jnp.einsum('bqhd,bsd->bqhs', qi_b.astype(f32), k_idx))
        score = jnp.einsum('bqh,bqhs->bqs', wi_b.astype(f32), rel)
        score = jnp.where(key_pos[None, None, :] <= q_pos[None, :, None], score, -jnp.inf)
        _, idx = lax.top_k(score, topk)
        k_sel = gather(k, idx)
        v_sel = gather(v, idx)
        dist = q_pos[None, :, None] - idx
        bias = rel_bias[t5_bucket(dist)].astype(f32)
        logits = jnp.einsum('bqhd,bqkhd->bqhk', q_b, k_sel).astype(f32) * (D ** -0.5) + jnp.moveaxis(bias, 3, 2)
        logits = jnp.where((dist >= 0)[:, :, None, :], logits, -jnp.inf)
        p = jax.nn.softmax(logits, axis=-1).astype(v.dtype)
        return jnp.einsum('bqhk,bqkhd->bqhd', p, v_sel)

    out = lax.map(attend_block, (blocks(q), blocks(q_idx), blocks(w_idx), jnp.arange(nb) * Q_BLOCK))
    return jnp.moveaxis(out, 0, 1).reshape(B, S, H * D)


def setup_inputs(seed: int = 0) -> dict:
    key = jax.random.key(seed)
    ks = jax.random.split(key, 16)
    f32 = jnp.float32

    def nrm(k, shape, scale):
        return jax.random.normal(k, shape, f32) * scale

    x = nrm(ks[0], (BATCH, SEQ, D_MODEL), 1.0)
    rel_bias = nrm(ks[1], (REL_BUCKETS, DSA_HEADS), 0.5)
    w_in = nrm(ks[2], (DEPTH, D_MODEL, D_IN), D_MODEL ** -0.5)
    conv_w = nrm(ks[3], (DEPTH, CONV_WIDTH, 3 * GDN_WIDTH), CONV_WIDTH ** -0.5)
    a_log = jnp.log(jax.random.uniform(ks[4], (DEPTH, GDN_HEADS), f32, 1.0, 16.0))
    dt = jnp.exp(jax.random.uniform(ks[5], (DEPTH, GDN_HEADS), f32, math.log(1e-3), math.log(1e-1)))
    dt_bias = dt + jnp.log(-jnp.expm1(-dt))
    gdn_norm_w = 1.0 + nrm(ks[6], (DEPTH, GDN_HEAD_DIM), 0.02)
    w_branch_a = nrm(ks[7], (DEPTH, GDN_WIDTH, D_MODEL), GDN_WIDTH ** -0.5)
    w_branch_b = nrm(ks[8], (DEPTH, DSA_WIDTH, D_MODEL), DSA_WIDTH ** -0.5)
    w_out = nrm(ks[9], (DEPTH, D_MODEL, D_MODEL), DEEPNORM_BETA * D_MODEL ** -0.5)
    ln1_g = 1.0 + nrm(ks[10], (DEPTH, D_MODEL), 0.02)
    ln1_b = nrm(ks[11], (DEPTH, D_MODEL), 0.02)
    w_ffn_in = nrm(ks[12], (DEPTH, D_MODEL, 2 * D_FF), D_MODEL ** -0.5)
    w_ffn_out = nrm(ks[13], (DEPTH, D_FF, D_MODEL), DEEPNORM_BETA * D_FF ** -0.5)
    ln2_g = 1.0 + nrm(ks[14], (DEPTH, D_MODEL), 0.02)
    ln2_b = nrm(ks[15], (DEPTH, D_MODEL), 0.02)
    return {'x': x, 'rel_bias': rel_bias, 'w_in': w_in, 'conv_w': conv_w, 'a_log': a_log,
            'dt_bias': dt_bias, 'gdn_norm_w': gdn_norm_w, 'w_branch_a': w_branch_a,
            'w_branch_b': w_branch_b, 'w_out': w_out, 'ln1_g': ln1_g, 'ln1_b': ln1_b,
            'w_ffn_in': w_ffn_in, 'w_ffn_out': w_ffn_out, 'ln2_g': ln2_g, 'ln2_b': ln2_b}


def reference(x, rel_bias, w_in, conv_w, a_log, dt_bias, gdn_norm_w, w_branch_a, w_branch_b,
              w_out, ln1_g, ln1_b, w_ffn_in, w_ffn_out, ln2_g, ln2_b):
    B, S, _ = x.shape
    offsets = [int(o) for o in np.cumsum(SPLIT_SIZES)[:-1]]

    def heads(t, n):
        return t.reshape(B, S, n, -1)

    for l in range(DEPTH):
        proj = x @ w_in[l]
        (qkv_a, a_in, b_in, z, q_b, k_b, v_b, q_i, k_i, w_i, gate_a, gate_b) = jnp.split(proj, offsets, axis=-1)

        qkv_a = causal_short_conv(qkv_a, conv_w[l])
        q_a, k_a, v_a = jnp.split(qkv_a, 3, axis=-1)
        log_decay = -jnp.exp(a_log[l]) * jax.nn.softplus(a_in + dt_bias[l])
        beta = jax.nn.sigmoid(b_in)
        o_a = gated_delta_rule_chunked(heads(q_a, GDN_HEADS), heads(k_a, GDN_HEADS),
                                       heads(v_a, GDN_HEADS), log_decay, beta)
        o_a = (o_a * lax.rsqrt(jnp.mean(jnp.square(o_a), axis=-1, keepdims=True) + RMS_EPS)
               * gdn_norm_w[l].astype(jnp.float32)
               * jax.nn.silu(heads(z, GDN_HEADS).astype(jnp.float32)))
        o_a = o_a.reshape(B, S, GDN_WIDTH).astype(x.dtype)

        o_b = dsa_sparse_attention(heads(q_b, DSA_HEADS), heads(k_b, DSA_HEADS), heads(v_b, DSA_HEADS),
                                   heads(q_i, IDX_HEADS), k_i, w_i, rel_bias)

        merged = jax.nn.sigmoid(gate_a) * (o_a @ w_branch_a[l]) + jax.nn.sigmoid(gate_b) * (o_b @ w_branch_b[l])
        x = layer_norm(DEEPNORM_ALPHA * x + merged @ w_out[l], ln1_g[l], ln1_b[l])

        h_gate, h_up = jnp.split(x @ w_ffn_in[l], 2, axis=-1)
        x = layer_norm(DEEPNORM_ALPHA * x + (jax.nn.silu(h_gate) * h_up) @ w_ffn_out[l], ln2_g[l], ln2_b[l])
    return x
```

```python
import functools
import math

import jax
import jax.numpy as jnp
from jax import lax
from jax.experimental import pallas as pl
from jax.experimental.pallas import tpu as pltpu

F32 = jnp.float32
BF16 = jnp.bfloat16

GDN_HEADS = 8
GDN_HEAD_DIM = 128
CONV_WIDTH = 4
CHUNK = 64
DSA_HEADS = 8
DSA_HEAD_DIM = 128
IDX_HEADS = 8
IDX_HEAD_DIM = 64
TOPK_MAX = 256
REL_BUCKETS = 32
REL_MAX_DIST = 128
LN_EPS = 1e-5
RMS_EPS = 1e-6

LANES = 128
VMEM_LIMIT = 56 * 1024 * 1024

NEG_BIG = -1e30
INT_MIN = -(2 ** 31)


def _params(sem):
    return pltpu.CompilerParams(dimension_semantics=sem, vmem_limit_bytes=VMEM_LIMIT)


def _pick(n, prefs):
    for p in prefs:
        if n % p == 0:
            return p
    return n


def _mm_kernel(x_ref, w_ref, o_ref):
    o_ref[...] = jnp.dot(x_ref[...], w_ref[...], preferred_element_type=F32).astype(o_ref.dtype)


def _matmul(x, w, out_dtype, name):
    m, k = x.shape
    n = w.shape[1]
    tm = _pick(m, (1024, 512, 256, 128))
    tn = _pick(n, (512, 256, 128))
    return pl.pallas_call(
        _mm_kernel,
        grid=(m // tm, n // tn),
        in_specs=[pl.BlockSpec((tm, k), lambda i, j: (i, 0)),
                  pl.BlockSpec((k, tn), lambda i, j: (0, j))],
        out_specs=pl.BlockSpec((tm, tn), lambda i, j: (i, j)),
        out_shape=jax.ShapeDtypeStruct((m, n), out_dtype),
        compiler_params=_params(("parallel", "parallel")),
        name=name,
    )(x, w)


def _ffn_in_kernel(x_ref, wg_ref, wu_ref, o_ref):
    x = x_ref[...]
    g = jnp.dot(x, wg_ref[...], preferred_element_type=F32)
    u = jnp.dot(x, wu_ref[...], preferred_element_type=F32)
    o_ref[...] = (g * jax.nn.sigmoid(g) * u).astype(o_ref.dtype)


def _ffn_in(x, w):
    m, k = x.shape
    f = w.shape[1] // 2
    tm = _pick(m, (1024, 512, 256, 128))
    tn = _pick(f, (512, 256, 128))
    nb = f // tn
    return pl.pallas_call(
        _ffn_in_kernel,
        grid=(m // tm, nb),
        in_specs=[pl.BlockSpec((tm, k), lambda i, j: (i, 0)),
                  pl.BlockSpec((k, tn), lambda i, j: (0, j)),
                  pl.BlockSpec((k, tn), lambda i, j: (0, j + nb))],
        out_specs=pl.BlockSpec((tm, tn), lambda i, j: (i, j)),
        out_shape=jax.ShapeDtypeStruct((m, f), BF16),
        compiler_params=_params(("parallel", "parallel")),
        name="ffn_in",
    )(x, w, w)


def _mm_res_ln_kernel(a_ref, w_ref, x_ref, g_ref, b_ref, of_ref, ob_ref, acc_ref, *, nk, alpha):
    k = pl.program_id(1)

    @pl.when(k == 0)
    def _():
        acc_ref[...] = jnp.zeros_like(acc_ref)

    acc_ref[...] += jnp.dot(a_ref[...], w_ref[...], preferred_element_type=F32)

    @pl.when(k == nk - 1)
    def _():
        y = alpha * x_ref[...] + acc_ref[...]
        mu = jnp.mean(y, axis=-1, keepdims=True)
        yc = y - mu
        var = jnp.mean(yc * yc, axis=-1, keepdims=True)
        out = yc * lax.rsqrt(var + LN_EPS) * g_ref[...] + b_ref[...]
        of_ref[...] = out
        ob_ref[...] = out.astype(BF16)


def _mm_res_ln(a, w, x, g, b, alpha, name):
    m, kdim = a.shape
    n = w.shape[1]
    tm = _pick(m, (512, 256, 128))
    tk = _pick(kdim, (1408, 1024, 512, 256, 128))
    nk = kdim // tk
    return pl.pallas_call(
        functools.partial(_mm_res_ln_kernel, nk=nk, alpha=alpha),
        grid=(m // tm, nk),
        in_specs=[pl.BlockSpec((tm, tk), lambda i, k: (i, k)),
                  pl.BlockSpec((tk, n), lambda i, k: (k, 0)),
                  pl.BlockSpec((tm, n), lambda i, k: (i, 0)),
                  pl.BlockSpec((1, n), lambda i, k: (0, 0)),
                  pl.BlockSpec((1, n), lambda i, k: (0, 0))],
        out_specs=[pl.BlockSpec((tm, n), lambda i, k: (i, 0)),
                   pl.BlockSpec((tm, n), lambda i, k: (i, 0))],
        out_shape=[jax.ShapeDtypeStruct((m, n), F32), jax.ShapeDtypeStruct((m, n), BF16)],
        scratch_shapes=[pltpu.VMEM((tm, n), F32)],
        compiler_params=_params(("parallel", "arbitrary")),
        name=name,
    )(a, w, x, g.reshape(1, n), b.reshape(1, n))


def _merge_kernel(oa_ref, ob_ref, wa_ref, wb_ref, ga_ref, gb_ref, o_ref):
    a = jnp.dot(oa_ref[...], wa_ref[...], preferred_element_type=F32)
    b = jnp.dot(ob_ref[...], wb_ref[...], preferred_element_type=F32)
    ga = jax.nn.sigmoid(ga_ref[...].astype(F32))
    gb = jax.nn.sigmoid(gb_ref[...].astype(F32))
    o_ref[...] = (ga * a + gb * b).astype(o_ref.dtype)


def _merge(o_a, o_b, w_a, w_b, p2, ga_col, gb_col, d_model):
    m, ka = o_a.shape
    kb = o_b.shape[1]
    tm = _pick(m, (1024, 512, 256, 128))
    tn = 512
    assert d_model % tn == 0 and ga_col % tn == 0 and gb_col % tn == 0
    ga_blk, gb_blk = ga_col // tn, gb_col // tn
    return pl.pallas_call(
        _merge_kernel,
        grid=(m // tm, d_model // tn),
        in_specs=[pl.BlockSpec((tm, ka), lambda i, j: (i, 0)),
                  pl.BlockSpec((tm, kb), lambda i, j: (i, 0)),
                  pl.BlockSpec((ka, tn), lambda i, j: (0, j)),
                  pl.BlockSpec((kb, tn), lambda i, j: (0, j)),
                  pl.BlockSpec((tm, tn), lambda i, j: (i, ga_blk + j)),
                  pl.BlockSpec((tm, tn), lambda i, j: (i, gb_blk + j))],
        out_specs=pl.BlockSpec((tm, tn), lambda i, j: (i, j)),
        out_shape=jax.ShapeDtypeStruct((m, d_model), BF16),
        compiler_params=_params(("parallel", "parallel")),
        name="merge",
    )(o_a, o_b, w_a, w_b, p2, p2)


def _split_dot(a, b, dims):
    a_hi = a.astype(BF16)
    b_hi = b.astype(BF16)
    a_lo = (a - a_hi.astype(F32)).astype(BF16)
    b_lo = (b - b_hi.astype(F32)).astype(BF16)
    d = functools.partial(jnp.einsum, dims, preferred_element_type=F32)
    return d(a_hi, b_hi) + (d(a_hi, b_lo) + d(a_lo, b_hi))


def _softplus(x):
    return jnp.maximum(x, 0.0) + jnp.log1p(jnp.exp(-jnp.abs(x)))


def _gdn_kernel(q_ref, k_ref, v_ref, z_ref, sm_ref, cwq_ref, cwk_ref, cwv_ref, alog_ref, dtb_ref, nw_ref,
                o_ref, ext_ref, halo_ref, state_ref, u_ref, w_ref, qd_ref, kdt_ref, in_ref, gt_ref, *, nc):
    h = pl.program_id(1)
    tb = pl.program_id(2)
    t = nc * CHUNK
    dk = GDN_HEAD_DIM

    @pl.when(tb == 0)
    def _():
        halo_ref[...] = jnp.zeros_like(halo_ref)
        state_ref[...] = jnp.zeros_like(state_ref)

    def conv_silu(x_ref, cw_ref, slot):
        x = x_ref[...]
        ext_ref[0:8, :] = halo_ref[slot]
        ext_ref[8:8 + t, :] = x
        halo_ref[slot] = x[t - 8:t, :]
        cw = cw_ref[...]
        y = cw[CONV_WIDTH - 1:CONV_WIDTH, :] * x
        for j in range(CONV_WIDTH - 1):
            y = y + cw[j:j + 1, :] * ext_ref[pl.ds(8 - (CONV_WIDTH - 1) + j, t), :]
        return y * jax.nn.sigmoid(y)

    q = conv_silu(q_ref, cwq_ref, 0)
    k = conv_silu(k_ref, cwk_ref, 1)
    v = conv_silu(v_ref, cwv_ref, 2)

    sm = sm_ref[...]
    lane = lax.broadcasted_iota(jnp.int32, sm.shape, 1)
    g_all = -jnp.exp(alog_ref[...]) * _softplus(sm + dtb_ref[...])
    beta_all = jax.nn.sigmoid(sm)
    g_col = jnp.sum(jnp.where(lane == h, g_all, 0.0), axis=1, keepdims=True)
    beta = jnp.sum(jnp.where(lane == GDN_HEADS + h, beta_all, 0.0), axis=1, keepdims=True)

    qn = q * lax.rsqrt(jnp.sum(q * q, axis=1, keepdims=True) + RMS_EPS) * (dk ** -0.5)
    kn = k * lax.rsqrt(jnp.sum(k * k, axis=1, keepdims=True) + RMS_EPS)
    kb = kn * beta
    vb = v * beta

    c3 = lambda a: a.reshape(nc, CHUNK, a.shape[-1])
    ri = lax.broadcasted_iota(jnp.int32, (CHUNK, CHUNK), 0)
    ci = lax.broadcasted_iota(jnp.int32, (CHUNK, CHUNK), 1)
    causal = (ci <= ri)[None]
    strict = (ci < ri)[None]
    eye = (ci == ri).astype(F32)[None]
    upper = (ri <= ci).astype(F32)[None]

    g3 = c3(g_col)
    gc_row = jnp.sum(g3 * upper, axis=1, keepdims=True)
    gc_col = jnp.sum(eye * gc_row, axis=2, keepdims=True)
    gc_last = gc_row[:, :, CHUNK - 1:CHUNK]
    decay = jnp.exp(jnp.where(causal, gc_col - gc_row, NEG_BIG))
    e_col = jnp.exp(gc_col)

    qn3, kn3, kb3, vb3 = c3(qn), c3(kn), c3(kb), c3(vb)
    kn3b = kn3.astype(BF16)
    kk = jnp.einsum("cid,cjd->cij", kb3.astype(BF16), kn3b, preferred_element_type=F32)
    mneg = jnp.where(strict, -(kk * decay), 0.0)
    inv = eye + mneg
    pw = mneg
    for _ in range(int(math.log2(CHUNK)) - 1):
        pw = _split_dot(pw, pw, "cij,cjk->cik")
        inv = inv + _split_dot(inv, pw, "cij,cjk->cik")
    u_ref[...] = _split_dot(inv, vb3, "cij,cjd->cid")
    w_ref[...] = _split_dot(inv, kb3 * e_col, "cij,cjd->cid")
    in_ref[...] = jnp.einsum("cid,cjd->cij", qn3.astype(BF16), kn3b, preferred_element_type=F32) * decay
    qd_ref[...] = qn3 * e_col
    kdt_ref[...] = jnp.swapaxes(kn3 * jnp.exp(gc_last - gc_col), 1, 2)
    gt_ref[...] = jnp.exp(gc_last)

    nw = nw_ref[...]
    for c in range(nc):
        s = state_ref[...]
        sb = s.astype(BF16)
        v_new = u_ref[c] - jnp.dot(w_ref[c].astype(BF16), sb, preferred_element_type=F32)
        vnb = v_new.astype(BF16)
        o = (jnp.dot(qd_ref[c].astype(BF16), sb, preferred_element_type=F32)
             + jnp.dot(in_ref[c].astype(BF16), vnb, preferred_element_type=F32))
        state_ref[...] = s * gt_ref[c] + jnp.dot(kdt_ref[c].astype(BF16), vnb, preferred_element_type=F32)
        zc = z_ref[c * CHUNK:(c + 1) * CHUNK, :]
        o = o * lax.rsqrt(jnp.mean(o * o, axis=1, keepdims=True) + RMS_EPS) * nw * (zc * jax.nn.sigmoid(zc))
        o_ref[c * CHUNK:(c + 1) * CHUNK, :] = o.astype(o_ref.dtype)


def _gdn(p1, ps, conv_w, a_log, dt_bias, norm_w, batch, seq):
    m = p1.shape[0]
    hd = GDN_HEAD_DIM
    nh = GDN_HEADS
    t = _pick(seq, (512, 256, 128, 64))
    nc = t // CHUNK
    nt = seq // t
    pad = lambda a: jnp.pad(a.astype(F32), (0, LANES - a.shape[0])).reshape(1, LANES)
    row = lambda b, h, i: b * nt + i
    col_spec = lambda off: pl.BlockSpec((t, hd), lambda b, h, i: (row(b, h, i), off + h))
    cw_spec = lambda off: pl.BlockSpec((CONV_WIDTH, hd), lambda b, h, i: (0, off + h))
    vec_spec = pl.BlockSpec((1, LANES), lambda b, h, i: (0, 0))
    return pl.pallas_call(
        functools.partial(_gdn_kernel, nc=nc),
        grid=(batch, nh, nt),
        in_specs=[col_spec(0), col_spec(nh), col_spec(2 * nh), col_spec(3 * nh),
                  pl.BlockSpec((t, LANES), lambda b, h, i: (row(b, h, i), 0)),
                  cw_spec(0), cw_spec(nh), cw_spec(2 * nh),
                  vec_spec, vec_spec, vec_spec],
        out_specs=pl.BlockSpec((t, hd), lambda b, h, i: (row(b, h, i), h)),
        out_shape=jax.ShapeDtypeStruct((m, nh * hd), BF16),
        scratch_shapes=[pltpu.VMEM((t + 8, hd), F32),
                        pltpu.VMEM((3, 8, hd), F32),
                        pltpu.VMEM((hd, hd), F32),
                        pltpu.VMEM((nc, CHUNK, hd), F32),
                        pltpu.VMEM((nc, CHUNK, hd), F32),
                        pltpu.VMEM((nc, CHUNK, hd), F32),
                        pltpu.VMEM((nc, hd, CHUNK), F32),
                        pltpu.VMEM((nc, CHUNK, CHUNK), F32),
                        pltpu.VMEM((nc, 1, 1), F32)],
        compiler_params=_params(("parallel", "parallel", "arbitrary")),
        name="gdn",
    )(p1, p1, p1, p1, ps, conv_w, conv_w, conv_w, pad(a_log), pad(dt_bias), norm_w.reshape(1, hd).astype(F32))


def _key_to_float(key):
    bits = jnp.where(key >= 0, key, key ^ jnp.int32(0x7FFFFFFF))
    return lax.bitcast_convert_type(bits, F32)


def _indexer_kernel(qi_ref, ws_ref, ki_ref, mask_ref, sc_ref, *, tq, tk, seq, topk, w_lane):
    i = pl.program_id(1)
    t0 = i * tq
    nt = (t0 + tq + tk - 1) // tk
    n_tiles = seq // tk
    sub = tk // LANES
    row = t0 + lax.broadcasted_iota(jnp.int32, (tq, tk), 0)
    col0 = lax.broadcasted_iota(jnp.int32, (tq, tk), 1)

    qi = qi_ref[...]
    ws = ws_ref[...]
    q_heads = [qi[:, h * IDX_HEAD_DIM:(h + 1) * IDX_HEAD_DIM] for h in range(IDX_HEADS)]
    w_cols = [ws[:, w_lane + h:w_lane + h + 1] for h in range(IDX_HEADS)]

    def score_tile(j, carry):
        s0 = pl.multiple_of(j * tk, tk)
        kt = ki_ref[pl.ds(s0, tk), :]
        acc = jnp.zeros((tq, tk), F32)
        for h in range(IDX_HEADS):
            r = lax.dot_general(q_heads[h], kt, (((1,), (1,)), ((), ())), preferred_element_type=F32)
            acc = acc + w_cols[h] * jnp.maximum(r, 0.0)
        sc_ref[:, pl.ds(s0, tk)] = jnp.where(col0 + s0 <= row, acc, -jnp.inf)
        return carry

    lax.fori_loop(0, nt, score_tile, 0)

    def count(pred):
        def body(j, acc):
            s0 = pl.multiple_of(j * tk, tk)
            for u in range(sub):
                x = sc_ref[:, pl.ds(s0 + u * LANES, LANES)]
                acc = acc + jnp.where(pred(x), 1.0, 0.0)
            return acc
        acc = lax.fori_loop(0, nt, body, jnp.zeros((tq, LANES), F32))
        return jnp.sum(acc, axis=1, keepdims=True)

    kf = jnp.float32(topk)
    c_nonneg = count(lambda x: x >= 0.0)
    base = jnp.where(c_nonneg >= kf, jnp.int32(0), jnp.int32(INT_MIN))

    def bit_step(b, base):
        cand = base | jnp.left_shift(jnp.int32(1), 30 - b)
        cf = _key_to_float(cand)
        cnt = count(lambda x: x >= cf)
        return jnp.where(cnt >= kf, cand, base)

    base = lax.fori_loop(0, 31, bit_step, base)
    take_all = base == jnp.int32(INT_MIN)
    thr = _key_to_float(base)
    n_ge = count(lambda x: x >= thr)
    tie_excess = jnp.logical_and(jnp.logical_not(take_all), n_ge > kf)
    any_excess = jnp.max(jnp.where(tie_excess, 1.0, 0.0)) > 0.0

    def write(j, sel):
        s0 = pl.multiple_of(j * tk, tk)
        keep = jnp.logical_and(col0 + s0 <= row, jnp.logical_or(take_all, sel))
        mask_ref[:, pl.ds(s0, tk)] = jnp.where(keep, 1, 0).astype(jnp.int8)

    @pl.when(jnp.logical_not(any_excess))
    def _():
        def body(j, carry):
            s0 = pl.multiple_of(j * tk, tk)
            write(j, sc_ref[:, pl.ds(s0, tk)] >= thr)
            return carry
        lax.fori_loop(0, nt, body, 0)

    @pl.when(any_excess)
    def _():
        need = kf - count(lambda x: x > thr)
        before = (lax.broadcasted_iota(jnp.int32, (tk, tk), 0)
                  < lax.broadcasted_iota(jnp.int32, (tk, tk), 1)).astype(BF16)

        def body(j, seen):
            s0 = pl.multiple_of(j * tk, tk)
            x = sc_ref[:, pl.ds(s0, tk)]
            eq = x == thr
            eqf = jnp.where(eq, 1.0, 0.0)
            rank = seen + jnp.dot(eqf.astype(BF16), before, preferred_element_type=F32)
            write(j, jnp.logical_or(x > thr, jnp.logical_and(eq, rank < need)))
            return seen + jnp.sum(eqf, axis=1, keepdims=True)
        lax.fori_loop(0, nt, body, jnp.zeros((tq, 1), F32))

    def clear(j, carry):
        s0 = pl.multiple_of(j * tk, tk)
        mask_ref[:, pl.ds(s0, tk)] = jnp.zeros((tq, tk), jnp.int8)
        return carry

    lax.fori_loop(nt, n_tiles, clear, 0)


def _indexer(p2, ps, ki, qi_col, w_lane, batch, seq):
    tq = _pick(seq, (256, 128))
    tk = _pick(seq, (512, 256, 128))
    nq = seq // tq
    qw = IDX_HEADS * IDX_HEAD_DIM
    assert qi_col % qw == 0
    topk = min(TOPK_MAX, seq // 4)
    return pl.pallas_call(
        functools.partial(_indexer_kernel, tq=tq, tk=tk, seq=seq, topk=topk, w_lane=w_lane),
        grid=(batch, nq),
        in_specs=[pl.BlockSpec((tq, qw), lambda b, i: (b * nq + i, qi_col // qw)),
                  pl.BlockSpec((tq, LANES), lambda b, i: (b * nq + i, 0)),
                  pl.BlockSpec((None, seq, IDX_HEAD_DIM), lambda b, i: (b, 0, 0))],
        out_specs=pl.BlockSpec((None, tq, seq), lambda b, i: (b, i, 0)),
        out_shape=jax.ShapeDtypeStruct((batch, seq, seq), jnp.int8),
        scratch_shapes=[pltpu.VMEM((tq, seq), F32)],
        compiler_params=_params(("parallel", "parallel")),
        name="indexer",
    )(p2, ps, ki)


def _bias_kernel(rb_ref, o_ref, *, t):
    d = pl.program_id(0)
    h = pl.program_id(1)
    r = lax.broadcasted_iota(jnp.int32, (t, t), 0)
    c = lax.broadcasted_iota(jnp.int32, (t, t), 1)
    n = jnp.maximum(r - c + d * t, 0)
    max_exact = REL_BUCKETS // 2
    log_ratio = jnp.log(jnp.maximum(n, 1).astype(F32) / max_exact) / math.log(REL_MAX_DIST / max_exact)
    large = max_exact + (log_ratio * (REL_BUCKETS - max_exact)).astype(jnp.int32)
    large = jnp.minimum(large, REL_BUCKETS - 1)
    bucket = jnp.where(n < max_exact, n, large)
    out = jnp.zeros((t, t), F32)
    for b in range(REL_BUCKETS):
        out = jnp.where(bucket == b, rb_ref[b, h], out)
    o_ref[...] = out


def _bias_tiles(rel_bias, t):
    assert t >= REL_MAX_DIST
    return pl.pallas_call(
        functools.partial(_bias_kernel, t=t),
        grid=(3, DSA_HEADS),
        in_specs=[pl.BlockSpec(memory_space=pltpu.SMEM)],
        out_specs=pl.BlockSpec((None, None, t, t), lambda d, h: (d, h, 0, 0)),
        out_shape=jax.ShapeDtypeStruct((3, DSA_HEADS, t, t), F32),
        compiler_params=_params(("parallel", "parallel")),
        name="rel_bias_tiles",
    )(rel_bias.astype(F32))


def _attn_kernel(q_ref, k_ref, v_ref, mask_ref, bias_ref, o_ref, m_ref, l_ref, acc_ref, *, nk, scale):
    i = pl.program_id(1)
    j = pl.program_id(2)
    hd = DSA_HEAD_DIM

    @pl.when(j == 0)
    def _():
        m_ref[...] = jnp.full_like(m_ref, NEG_BIG)
        l_ref[...] = jnp.zeros_like(l_ref)
        acc_ref[...] = jnp.zeros_like(acc_ref)

    @pl.when(j <= i)
    def _():
        keep = mask_ref[...].astype(jnp.int32) != 0
        kind = jnp.minimum(i - j, 2)
        for h in range(DSA_HEADS):
            sl = slice(h * hd, (h + 1) * hd)
            s = lax.dot_general(q_ref[:, sl], k_ref[:, sl], (((1,), (1,)), ((), ())),
                                preferred_element_type=F32)
            s = jnp.where(keep, s * scale + bias_ref[kind, h], NEG_BIG)
            m_prev = m_ref[h]
            m_new = jnp.maximum(m_prev, jnp.max(s, axis=1, keepdims=True))
            alpha = jnp.exp(m_prev - m_new)
            p = jnp.where(keep, jnp.exp(s - m_new), 0.0)
            l_ref[h] = alpha * l_ref[h] + jnp.sum(p, axis=1, keepdims=True)
            acc_ref[:, sl] = alpha * acc_ref[:, sl] + jnp.dot(p.astype(BF16), v_ref[:, sl],
                                                              preferred_element_type=F32)
            m_ref[h] = m_new

    @pl.when(j == nk - 1)
    def _():
        for h in range(DSA_HEADS):
            sl = slice(h * hd, (h + 1) * hd)
            o_ref[:, sl] = (acc_ref[:, sl] / l_ref[h]).astype(o_ref.dtype)


def _attention(p2, mask, bias, q_col, k_col, v_col, batch, seq, t):
    m = p2.shape[0]
    width = DSA_HEADS * DSA_HEAD_DIM
    n = seq // t
    assert q_col % width == 0 and k_col % width == 0 and v_col % width == 0
    qb, kb, vb = q_col // width, k_col // width, v_col // width
    return pl.pallas_call(
        functools.partial(_attn_kernel, nk=n, scale=DSA_HEAD_DIM ** -0.5),
        grid=(batch, n, n),
        in_specs=[pl.BlockSpec((t, width), lambda b, i, j: (b * n + i, qb)),
                  pl.BlockSpec((t, width), lambda b, i, j: (b * n + jnp.minimum(j, i), kb)),
                  pl.BlockSpec((t, width), lambda b, i, j: (b * n + jnp.minimum(j, i), vb)),
                  pl.BlockSpec((None, t, t), lambda b, i, j: (b, i, jnp.minimum(j, i))),
                  pl.BlockSpec((3, DSA_HEADS, t, t), lambda b, i, j: (0, 0, 0, 0))],
        out_specs=pl.BlockSpec((t, width), lambda b, i, j: (b * n + i, 0)),
        out_shape=jax.ShapeDtypeStruct((m, width), BF16),
        scratch_shapes=[pltpu.VMEM((DSA_HEADS, t, 1), F32),
                        pltpu.VMEM((DSA_HEADS, t, 1), F32),
                        pltpu.VMEM((t, width), F32)],
        compiler_params=_params(("parallel", "parallel", "arbitrary")),
        name="sparse_attn",
    )(p2, p2, p2, mask, bias)


def _split_w_in(w, d_model):
    gw = GDN_HEADS * GDN_HEAD_DIM
    dw = DSA_HEADS * DSA_HEAD_DIM
    iw = IDX_HEADS * IDX_HEAD_DIM
    sizes = (3 * gw, GDN_HEADS, GDN_HEADS, gw, dw, dw, dw, iw, IDX_HEAD_DIM, IDX_HEADS, d_model, d_model)
    offs = [0]
    for s in sizes:
        offs.append(offs[-1] + s)
    assert offs[-1] == w.shape[1]
    part = lambda i: w[:, offs[i]:offs[i + 1]]
    qkv_a, a_in, b_in, z, q_b, k_b, v_b, q_i, k_i, w_i, gate_a, gate_b = (part(i) for i in range(12))
    w_gdn = jnp.concatenate([qkv_a, z], axis=1).astype(BF16)
    fill = jnp.zeros((w.shape[0], LANES - 2 * GDN_HEADS - IDX_HEADS - IDX_HEAD_DIM), w.dtype)
    w_small = jnp.concatenate([a_in, b_in, w_i, fill, k_i], axis=1).astype(BF16)
    w_rest = jnp.concatenate([q_b, k_b, v_b, q_i, gate_a, gate_b], axis=1).astype(BF16)
    cols = dict(q=0, k=dw, v=2 * dw, qi=3 * dw, ga=3 * dw + iw, gb=3 * dw + iw + d_model,
                w_lane=2 * GDN_HEADS, ki_lane=LANES - IDX_HEAD_DIM)
    return w_gdn, w_small, w_rest, cols


@jax.jit
def _forward(x, rel_bias, w_in, conv_w, a_log, dt_bias, gdn_norm_w, w_branch_a, w_branch_b, w_out,
             ln1_g, ln1_b, w_ffn_in, w_ffn_out, ln2_g, ln2_b):
    batch, seq, d_model = x.shape
    depth = w_in.shape[0]
    m = batch * seq
    alpha = (2 * depth) ** 0.25
    t_attn = _pick(seq, (256, 128))
    bias = _bias_tiles(rel_bias, t_attn)
    xf = x.reshape(m, d_model).astype(F32)
    xb = xf.astype(BF16)
    for l in range(depth):
        w_gdn, w_small, w_rest, cols = _split_w_in(w_in[l], d_model)
        p1 = _matmul(xb, w_gdn, F32, "proj_gdn")
        ps = _matmul(xb, w_small, F32, "proj_small")
        p2 = _matmul(xb, w_rest, BF16, "proj_rest")
        o_a = _gdn(p1, ps, conv_w[l], a_log[l], dt_bias[l], gdn_norm_w[l], batch, seq)
        ki = ps[:, cols["ki_lane"]:].astype(BF16).reshape(batch, seq, IDX_HEAD_DIM)
        mask = _indexer(p2, ps, ki, cols["qi"], cols["w_lane"], batch, seq)
        o_b = _attention(p2, mask, bias, cols["q"], cols["k"], cols["v"], batch, seq, t_attn)
        merged = _merge(o_a, o_b, w_branch_a[l].astype(BF16), w_branch_b[l].astype(BF16), p2,
                        cols["ga"], cols["gb"], d_model)
        xf, xb = _mm_res_ln(merged, w_out[l].astype(BF16), xf, ln1_g[l], ln1_b[l], alpha, "out_proj_ln")
        act = _ffn_in(xb, w_ffn_in[l].astype(BF16))
        xf, xb = _mm_res_ln(act, w_ffn_out[l].astype(BF16), xf, ln2_g[l], ln2_b[l], alpha, "ffn_out_ln")
    return xf.reshape(batch, seq, d_model).astype(x.dtype)


def kernel(x, rel_bias, w_in, conv_w, a_log, dt_bias, gdn_norm_w, w_branch_a, w_branch_b, w_out, ln1_g, ln1_b,
           w_ffn_in, w_ffn_out, ln2_g, ln2_b):
    return _forward(x, rel_bias, w_in, conv_w, a_log, dt_bias, gdn_norm_w, w_branch_a, w_branch_b, w_out,
                    ln1_g, ln1_b, w_ffn_in, w_ffn_out, ln2_g, ln2_b)
```

```python
import functools
import math

import jax
import jax.numpy as jnp
from jax import lax
from jax.experimental import pallas as pl
from jax.experimental.pallas import tpu as pltpu

F32 = jnp.float32
BF16 = jnp.bfloat16

GDN_HEADS = 8
GDN_HEAD_DIM = 128
CONV_WIDTH = 4
CHUNK = 64
DSA_HEADS = 8
DSA_HEAD_DIM = 128
IDX_HEADS = 8
IDX_HEAD_DIM = 64
TOPK_MAX = 256
REL_BUCKETS = 32
REL_MAX_DIST = 128
LN_EPS = 1e-5
RMS_EPS = 1e-6

LANES = 128
VMEM_LIMIT = 56 * 1024 * 1024

NEG_BIG = -1e30
INT_MIN = -(2 ** 31)
LOG2E = math.log2(math.e)
BIAS_T = 256


def _params(sem):
    return pltpu.CompilerParams(dimension_semantics=sem, vmem_limit_bytes=VMEM_LIMIT)


def _pick(n, prefs):
    for p in prefs:
        if n % p == 0:
            return p
    return n


def _mm_kernel(x_ref, w_ref, o_ref):
    o_ref[...] = jnp.dot(x_ref[...], w_ref[...], preferred_element_type=F32).astype(o_ref.dtype)


def _matmul(x, w, out_dtype, name):
    m, k = x.shape
    n = w.shape[1]
    tm = _pick(m, (1024, 512, 256, 128))
    tn = _pick(n, (512, 256, 128))
    return pl.pallas_call(
        _mm_kernel,
        grid=(m // tm, n // tn),
        in_specs=[pl.BlockSpec((tm, k), lambda i, j: (i, 0)),
                  pl.BlockSpec((k, tn), lambda i, j: (0, j))],
        out_specs=pl.BlockSpec((tm, tn), lambda i, j: (i, j)),
        out_shape=jax.ShapeDtypeStruct((m, n), out_dtype),
        compiler_params=_params(("parallel", "parallel")),
        name=name,
    )(x, w)


def _ffn_in_kernel(x_ref, wg_ref, wu_ref, o_ref):
    x = x_ref[...]
    g = jnp.dot(x, wg_ref[...], preferred_element_type=F32)
    u = jnp.dot(x, wu_ref[...], preferred_element_type=F32)
    o_ref[...] = (g * jax.nn.sigmoid(g) * u).astype(o_ref.dtype)


def _ffn_in(x, w):
    m, k = x.shape
    f = w.shape[1] // 2
    tm = _pick(m, (1024, 512, 256, 128))
    tn = _pick(f, (512, 256, 128))
    nb = f // tn
    return pl.pallas_call(
        _ffn_in_kernel,
        grid=(m // tm, nb),
        in_specs=[pl.BlockSpec((tm, k), lambda i, j: (i, 0)),
                  pl.BlockSpec((k, tn), lambda i, j: (0, j)),
                  pl.BlockSpec((k, tn), lambda i, j: (0, j + nb))],
        out_specs=pl.BlockSpec((tm, tn), lambda i, j: (i, j)),
        out_shape=jax.ShapeDtypeStruct((m, f), BF16),
        compiler_params=_params(("parallel", "parallel")),
        name="ffn_in",
    )(x, w, w)


def _mm_res_ln_kernel(a_ref, w_ref, x_ref, g_ref, b_ref, of_ref, ob_ref, acc_ref, *, nk, alpha):
    k = pl.program_id(1)

    @pl.when(k == 0)
    def _():
        acc_ref[...] = jnp.zeros_like(acc_ref)

    acc_ref[...] += jnp.dot(a_ref[...], w_ref[...], preferred_element_type=F32)

    @pl.when(k == nk - 1)
    def _():
        y = alpha * x_ref[...] + acc_ref[...]
        mu = jnp.mean(y, axis=-1, keepdims=True)
        yc = y - mu
        var = jnp.mean(yc * yc, axis=-1, keepdims=True)
        out = yc * lax.rsqrt(var + LN_EPS) * g_ref[...] + b_ref[...]
        of_ref[...] = out
        ob_ref[...] = out.astype(BF16)


def _mm_res_ln(a, w, x, g, b, alpha, name):
    m, kdim = a.shape
    n = w.shape[1]
    tm = _pick(m, (512, 256, 128))
    tk = _pick(kdim, (1408, 1024, 512, 256, 128))
    nk = kdim // tk
    return pl.pallas_call(
        functools.partial(_mm_res_ln_kernel, nk=nk, alpha=alpha),
        grid=(m // tm, nk),
        in_specs=[pl.BlockSpec((tm, tk), lambda i, k: (i, k)),
                  pl.BlockSpec((tk, n), lambda i, k: (k, 0)),
                  pl.BlockSpec((tm, n), lambda i, k: (i, 0)),
                  pl.BlockSpec((1, n), lambda i, k: (0, 0)),
                  pl.BlockSpec((1, n), lambda i, k: (0, 0))],
        out_specs=[pl.BlockSpec((tm, n), lambda i, k: (i, 0)),
                   pl.BlockSpec((tm, n), lambda i, k: (i, 0))],
        out_shape=[jax.ShapeDtypeStruct((m, n), F32), jax.ShapeDtypeStruct((m, n), BF16)],
        scratch_shapes=[pltpu.VMEM((tm, n), F32)],
        compiler_params=_params(("parallel", "arbitrary")),
        name=name,
    )(a, w, x, g.reshape(1, n), b.reshape(1, n))


def _merge_kernel(oa_ref, ob_ref, wa_ref, wb_ref, ga_ref, gb_ref, o_ref):
    a = jnp.dot(oa_ref[...], wa_ref[...], preferred_element_type=F32)
    b = jnp.dot(ob_ref[...], wb_ref[...], preferred_element_type=F32)
    ga = jax.nn.sigmoid(ga_ref[...].astype(F32))
    gb = jax.nn.sigmoid(gb_ref[...].astype(F32))
    o_ref[...] = (ga * a + gb * b).astype(o_ref.dtype)


def _merge(o_a, o_b, w_a, w_b, p2, ga_col, gb_col, d_model):
    m, ka = o_a.shape
    kb = o_b.shape[1]
    tm = _pick(m, (1024, 512, 256, 128))
    tn = 512
    assert d_model % tn == 0 and ga_col % tn == 0 and gb_col % tn == 0
    ga_blk, gb_blk = ga_col // tn, gb_col // tn
    return pl.pallas_call(
        _merge_kernel,
        grid=(m // tm, d_model // tn),
        in_specs=[pl.BlockSpec((tm, ka), lambda i, j: (i, 0)),
                  pl.BlockSpec((tm, kb), lambda i, j: (i, 0)),
                  pl.BlockSpec((ka, tn), lambda i, j: (0, j)),
                  pl.BlockSpec((kb, tn), lambda i, j: (0, j)),
                  pl.BlockSpec((tm, tn), lambda i, j: (i, ga_blk + j)),
                  pl.BlockSpec((tm, tn), lambda i, j: (i, gb_blk + j))],
        out_specs=pl.BlockSpec((tm, tn), lambda i, j: (i, j)),
        out_shape=jax.ShapeDtypeStruct((m, d_model), BF16),
        compiler_params=_params(("parallel", "parallel")),
        name="merge",
    )(o_a, o_b, w_a, w_b, p2, p2)


def _split_dot(a, b, dims):
    a_hi = a.astype(BF16)
    b_hi = b.astype(BF16)
    a_lo = (a - a_hi.astype(F32)).astype(BF16)
    b_lo = (b - b_hi.astype(F32)).astype(BF16)
    d = functools.partial(jnp.einsum, dims, preferred_element_type=F32)
    return d(a_hi, b_hi) + (d(a_hi, b_lo) + d(a_lo, b_hi))


def _softplus(x):
    return jnp.maximum(x, 0.0) + jnp.log1p(jnp.exp(-jnp.abs(x)))


def _gdn_kernel(q_ref, k_ref, v_ref, z_ref, sm_ref, cwq_ref, cwk_ref, cwv_ref, alog_ref, dtb_ref, nw_ref,
                o_ref, ext_ref, halo_ref, state_ref, u_ref, w_ref, qd_ref, kdt_ref, in_ref, gt_ref, *, nc):
    h = pl.program_id(1)
    tb = pl.program_id(2)
    t = nc * CHUNK
    dk = GDN_HEAD_DIM

    @pl.when(tb == 0)
    def _():
        halo_ref[...] = jnp.zeros_like(halo_ref)
        state_ref[...] = jnp.zeros_like(state_ref)

    def conv_silu(x_ref, cw_ref, slot):
        x = x_ref[...]
        ext_ref[0:8, :] = halo_ref[slot]
        ext_ref[8:8 + t, :] = x
        halo_ref[slot] = x[t - 8:t, :]
        cw = cw_ref[...]
        y = cw[CONV_WIDTH - 1:CONV_WIDTH, :] * x
        for j in range(CONV_WIDTH - 1):
            y = y + cw[j:j + 1, :] * ext_ref[pl.ds(8 - (CONV_WIDTH - 1) + j, t), :]
        return y * jax.nn.sigmoid(y)

    q = conv_silu(q_ref, cwq_ref, 0)
    k = conv_silu(k_ref, cwk_ref, 1)
    v = conv_silu(v_ref, cwv_ref, 2)

    sm = sm_ref[...]
    lane = lax.broadcasted_iota(jnp.int32, sm.shape, 1)
    g_all = -jnp.exp(alog_ref[...]) * _softplus(sm + dtb_ref[...])
    beta_all = jax.nn.sigmoid(sm)
    g_col = jnp.sum(jnp.where(lane == h, g_all, 0.0), axis=1, keepdims=True)
    beta = jnp.sum(jnp.where(lane == GDN_HEADS + h, beta_all, 0.0), axis=1, keepdims=True)

    qn = q * lax.rsqrt(jnp.sum(q * q, axis=1, keepdims=True) + RMS_EPS) * (dk ** -0.5)
    kn = k * lax.rsqrt(jnp.sum(k * k, axis=1, keepdims=True) + RMS_EPS)
    kb = kn * beta
    vb = v * beta

    c3 = lambda a: a.reshape(nc, CHUNK, a.shape[-1])
    ri = lax.broadcasted_iota(jnp.int32, (CHUNK, CHUNK), 0)
    ci = lax.broadcasted_iota(jnp.int32, (CHUNK, CHUNK), 1)
    causal = (ci <= ri)[None]
    strict = (ci < ri)[None]
    eye = (ci == ri).astype(F32)[None]
    upper = (ri <= ci).astype(F32)[None]

    g3 = c3(g_col)
    gc_row = jnp.sum(g3 * upper, axis=1, keepdims=True)
    gc_col = jnp.sum(eye * gc_row, axis=2, keepdims=True)
    gc_last = gc_row[:, :, CHUNK - 1:CHUNK]
    decay = jnp.exp(jnp.where(causal, gc_col - gc_row, NEG_BIG))
    e_col = jnp.exp(gc_col)

    qn3, kn3, kb3, vb3 = c3(qn), c3(kn), c3(kb), c3(vb)
    kn3b = kn3.astype(BF16)
    kk = jnp.einsum("cid,cjd->cij", kb3.astype(BF16), kn3b, preferred_element_type=F32)
    mneg = jnp.where(strict, -(kk * decay), 0.0)
    inv = eye + mneg
    pw = mneg
    for _ in range(int(math.log2(CHUNK)) - 1):
        pw = _split_dot(pw, pw, "cij,cjk->cik")
        inv = inv + _split_dot(inv, pw, "cij,cjk->cik")
    u_ref[...] = _split_dot(inv, vb3, "cij,cjd->cid")
    w_ref[...] = _split_dot(inv, kb3 * e_col, "cij,cjd->cid")
    in_ref[...] = jnp.einsum("cid,cjd->cij", qn3.astype(BF16), kn3b, preferred_element_type=F32) * decay
    qd_ref[...] = qn3 * e_col
    kdt_ref[...] = jnp.swapaxes(kn3 * jnp.exp(gc_last - gc_col), 1, 2)
    gt_ref[...] = jnp.exp(gc_last)

    nw = nw_ref[...]
    for c in range(nc):
        s = state_ref[...]
        sb = s.astype(BF16)
        v_new = u_ref[c] - jnp.dot(w_ref[c].astype(BF16), sb, preferred_element_type=F32)
        vnb = v_new.astype(BF16)
        o = (jnp.dot(qd_ref[c].astype(BF16), sb, preferred_element_type=F32)
             + jnp.dot(in_ref[c].astype(BF16), vnb, preferred_element_type=F32))
        state_ref[...] = s * gt_ref[c] + jnp.dot(kdt_ref[c].astype(BF16), vnb, preferred_element_type=F32)
        zc = z_ref[c * CHUNK:(c + 1) * CHUNK, :]
        o = o * lax.rsqrt(jnp.mean(o * o, axis=1, keepdims=True) + RMS_EPS) * nw * (zc * jax.nn.sigmoid(zc))
        o_ref[c * CHUNK:(c + 1) * CHUNK, :] = o.astype(o_ref.dtype)


def _gdn(p1, ps, conv_w, a_log, dt_bias, norm_w, batch, seq):
    m = p1.shape[0]
    hd = GDN_HEAD_DIM
    nh = GDN_HEADS
    t = _pick(seq, (512, 256, 128, 64))
    nc = t // CHUNK
    nt = seq // t
    pad = lambda a: jnp.pad(a.astype(F32), (0, LANES - a.shape[0])).reshape(1, LANES)
    row = lambda b, h, i: b * nt + i
    col_spec = lambda off: pl.BlockSpec((t, hd), lambda b, h, i: (row(b, h, i), off + h))
    cw_spec = lambda off: pl.BlockSpec((CONV_WIDTH, hd), lambda b, h, i: (0, off + h))
    vec_spec = pl.BlockSpec((1, LANES), lambda b, h, i: (0, 0))
    return pl.pallas_call(
        functools.partial(_gdn_kernel, nc=nc),
        grid=(batch, nh, nt),
        in_specs=[col_spec(0), col_spec(nh), col_spec(2 * nh), col_spec(3 * nh),
                  pl.BlockSpec((t, LANES), lambda b, h, i: (row(b, h, i), 0)),
                  cw_spec(0), cw_spec(nh), cw_spec(2 * nh),
                  vec_spec, vec_spec, vec_spec],
        out_specs=pl.BlockSpec((t, hd), lambda b, h, i: (row(b, h, i), h)),
        out_shape=jax.ShapeDtypeStruct((m, nh * hd), BF16),
        scratch_shapes=[pltpu.VMEM((t + 8, hd), F32),
                        pltpu.VMEM((3, 8, hd), F32),
                        pltpu.VMEM((hd, hd), F32),
                        pltpu.VMEM((nc, CHUNK, hd), F32),
                        pltpu.VMEM((nc, CHUNK, hd), F32),
                        pltpu.VMEM((nc, CHUNK, hd), F32),
                        pltpu.VMEM((nc, hd, CHUNK), F32),
                        pltpu.VMEM((nc, CHUNK, CHUNK), F32),
                        pltpu.VMEM((nc, 1, 1), F32)],
        compiler_params=_params(("parallel", "parallel", "arbitrary")),
        name="gdn",
    )(p1, p1, p1, p1, ps, conv_w, conv_w, conv_w, pad(a_log), pad(dt_bias), norm_w.reshape(1, hd).astype(F32))


def _key_to_float(key):
    bits = jnp.where(key >= 0, key, key ^ jnp.int32(0x7FFFFFFF))
    return lax.bitcast_convert_type(bits, F32)


def _indexer_kernel(qi_ref, ws_ref, ki_ref, mask_ref, sc_ref, base_ref, nge_ref, *, tq, tk, rb, seq, topk,
                    w_lane):
    i = pl.program_id(1)
    t0 = i * tq
    nt = (t0 + tq + tk - 1) // tk
    n_tiles = seq // tk
    sub = tk // LANES
    row = t0 + lax.broadcasted_iota(jnp.int32, (tq, tk), 0)
    col0 = lax.broadcasted_iota(jnp.int32, (tq, tk), 1)

    qi = qi_ref[...]
    ws = ws_ref[...]
    q_heads = [qi[:, h * IDX_HEAD_DIM:(h + 1) * IDX_HEAD_DIM] for h in range(IDX_HEADS)]
    w_cols = [ws[:, w_lane + h:w_lane + h + 1] for h in range(IDX_HEADS)]

    def score_tile(j, carry):
        s0 = pl.multiple_of(j * tk, tk)
        kt = ki_ref[pl.ds(s0, tk), :]
        acc = jnp.zeros((tq, tk), F32)
        for h in range(IDX_HEADS):
            r = lax.dot_general(q_heads[h], kt, (((1,), (1,)), ((), ())), preferred_element_type=F32)
            acc = acc + w_cols[h] * jnp.maximum(r, 0.0)
        sc_ref[:, pl.ds(s0, tk)] = jnp.where(col0 + s0 <= row, acc, -jnp.inf)
        return carry

    lax.fori_loop(0, nt, score_tile, 0)

    def count_rows(r0, rows, pred):
        def body(j, acc):
            s0 = pl.multiple_of(j * tk, tk)
            for u in range(sub):
                x = sc_ref[pl.ds(r0, rows), pl.ds(s0 + u * LANES, LANES)]
                acc = acc + jnp.where(pred(x), 1.0, 0.0)
            return acc
        acc = lax.fori_loop(0, nt, body, jnp.zeros((rows, LANES), F32))
        return jnp.sum(acc, axis=1, keepdims=True)

    kf = jnp.float32(topk)
    lanes = lambda a: jnp.broadcast_to(a, (rb, LANES))

    def search_block(r, carry):
        r0 = pl.multiple_of(r * rb, rb)
        c_nonneg = count_rows(r0, rb, lambda x: x >= 0.0)
        nonneg = c_nonneg >= kf
        base0 = jnp.where(nonneg, jnp.int32(0), jnp.int32(INT_MIN))
        cnt0 = jnp.where(nonneg, c_nonneg, jnp.float32(seq + 1))

        def unsettled(carry):
            b, _, cnt = carry
            return jnp.logical_and(b < 31, jnp.max(jnp.where(cnt != kf, 1.0, 0.0)) > 0.0)

        def bit_step(carry):
            b, base, cnt = carry
            cand = base | jnp.left_shift(jnp.int32(1), 30 - b)
            cf = lanes(_key_to_float(cand))
            c = count_rows(r0, rb, lambda x: x >= cf)
            ok = c >= kf
            return b + 1, jnp.where(ok, cand, base), jnp.where(ok, c, cnt)

        _, base, cnt = lax.while_loop(unsettled, bit_step, (jnp.int32(0), base0, cnt0))
        base_ref[pl.ds(r0, rb), :] = base
        nge_ref[pl.ds(r0, rb), :] = cnt
        return carry

    lax.fori_loop(0, tq // rb, search_block, 0)
    base = base_ref[...]
    take_all = base == jnp.int32(INT_MIN)
    thr = _key_to_float(base)
    tie_excess = jnp.logical_and(jnp.logical_not(take_all), nge_ref[...] > kf)
    any_excess = jnp.max(jnp.where(tie_excess, 1.0, 0.0)) > 0.0

    def write(j, sel):
        s0 = pl.multiple_of(j * tk, tk)
        keep = jnp.logical_and(col0 + s0 <= row, jnp.logical_or(take_all, sel))
        mask_ref[:, pl.ds(s0, tk)] = jnp.where(keep, 1, 0).astype(jnp.int8)

    @pl.when(jnp.logical_not(any_excess))
    def _():
        def body(j, carry):
            s0 = pl.multiple_of(j * tk, tk)
            write(j, sc_ref[:, pl.ds(s0, tk)] >= thr)
            return carry
        lax.fori_loop(0, nt, body, 0)

    @pl.when(any_excess)
    def _():
        need = kf - count_rows(0, tq, lambda x: x > thr)
        before = (lax.broadcasted_iota(jnp.int32, (tk, tk), 0)
                  < lax.broadcasted_iota(jnp.int32, (tk, tk), 1)).astype(BF16)

        def body(j, seen):
            s0 = pl.multiple_of(j * tk, tk)
            x = sc_ref[:, pl.ds(s0, tk)]
            eq = x == thr
            eqf = jnp.where(eq, 1.0, 0.0)
            rank = seen + jnp.dot(eqf.astype(BF16), before, preferred_element_type=F32)
            write(j, jnp.logical_or(x > thr, jnp.logical_and(eq, rank < need)))
            return seen + jnp.sum(eqf, axis=1, keepdims=True)
        lax.fori_loop(0, nt, body, jnp.zeros((tq, 1), F32))

    def clear(j, carry):
        s0 = pl.multiple_of(j * tk, tk)
        mask_ref[:, pl.ds(s0, tk)] = jnp.zeros((tq, tk), jnp.int8)
        return carry

    lax.fori_loop(nt, n_tiles, clear, 0)


def _indexer(p2, ps, ki, qi_col, w_lane, batch, seq):
    tq = _pick(seq, (256, 128))
    tk = _pick(seq, (512, 256, 128))
    nq = seq // tq
    qw = IDX_HEADS * IDX_HEAD_DIM
    assert qi_col % qw == 0
    topk = min(TOPK_MAX, seq // 4)
    return pl.pallas_call(
        functools.partial(_indexer_kernel, tq=tq, tk=tk, rb=min(tq, 128), seq=seq, topk=topk, w_lane=w_lane),
        grid=(batch, nq),
        in_specs=[pl.BlockSpec((tq, qw), lambda b, i: (b * nq + i, qi_col // qw)),
                  pl.BlockSpec((tq, LANES), lambda b, i: (b * nq + i, 0)),
                  pl.BlockSpec((None, seq, IDX_HEAD_DIM), lambda b, i: (b, 0, 0))],
        out_specs=pl.BlockSpec((None, tq, seq), lambda b, i: (b, i, 0)),
        out_shape=jax.ShapeDtypeStruct((batch, seq, seq), jnp.int8),
        scratch_shapes=[pltpu.VMEM((tq, seq), F32),
                        pltpu.VMEM((tq, 1), jnp.int32),
                        pltpu.VMEM((tq, 1), F32)],
        compiler_params=_params(("parallel", "parallel")),
        name="indexer",
    )(p2, ps, ki)


def _bias_kernel(rb_ref, o_ref, *, t):
    d = pl.program_id(0)
    h = pl.program_id(1)
    r = lax.broadcasted_iota(jnp.int32, (t, t), 0)
    c = lax.broadcasted_iota(jnp.int32, (t, t), 1)
    n = jnp.maximum(r - c + d * t, 0)
    max_exact = REL_BUCKETS // 2
    log_ratio = jnp.log(jnp.maximum(n, 1).astype(F32) / max_exact) / math.log(REL_MAX_DIST / max_exact)
    large = max_exact + (log_ratio * (REL_BUCKETS - max_exact)).astype(jnp.int32)
    large = jnp.minimum(large, REL_BUCKETS - 1)
    bucket = jnp.where(n < max_exact, n, large)
    out = jnp.zeros((t, t), F32)
    for b in range(REL_BUCKETS):
        out = jnp.where(bucket == b, rb_ref[b, h], out)
    o_ref[...] = out * LOG2E


def _bias_tiles(rel_bias, t):
    assert t >= REL_MAX_DIST
    return pl.pallas_call(
        functools.partial(_bias_kernel, t=t),
        grid=(3, DSA_HEADS),
        in_specs=[pl.BlockSpec(memory_space=pltpu.SMEM)],
        out_specs=pl.BlockSpec((None, None, t, t), lambda d, h: (d, h, 0, 0)),
        out_shape=jax.ShapeDtypeStruct((3, DSA_HEADS, t, t), F32),
        compiler_params=_params(("parallel", "parallel")),
        name="rel_bias_tiles",
    )(rel_bias.astype(F32))


def _attn_kernel(qi_ref, kj_ref, q_ref, k_ref, v_ref, mask_ref, bias_ref, o_ref,
                 m_ref, acc_ref, vaug_ref, madd_ref, *, tq, tk):
    pair = pl.program_id(1)
    i = qi_ref[pair]
    j = kj_ref[pair]
    hd = DSA_HEAD_DIM
    bt = BIAS_T
    first = j == 0
    last = j == (i * tq + tq - 1) // tk

    @pl.when(pair == 0)
    def _():
        vaug_ref[...] = jnp.ones_like(vaug_ref)

    @pl.when(first)
    def _():
        m_ref[...] = jnp.full_like(m_ref, NEG_BIG)
        acc_ref[...] = jnp.zeros_like(acc_ref)

    madd_ref[...] = jnp.where(mask_ref[...].astype(jnp.int32) != 0, 0.0, NEG_BIG)
    for h in range(DSA_HEADS):
        vaug_ref[:, 2 * h * hd:(2 * h + 1) * hd] = v_ref[:, h * hd:(h + 1) * hd]

    kinds = [jnp.clip((i * tq - j * tk) // bt - c, 0, 2) for c in range(tk // bt)]
    for h in range(DSA_HEADS):
        sl = slice(h * hd, (h + 1) * hd)
        s = lax.dot_general(q_ref[:, sl], k_ref[:, sl], (((1,), (1,)), ((), ())),
                            preferred_element_type=F32)
        cols = []
        for c in range(tk // LANES):
            cs = slice(c * LANES, (c + 1) * LANES)
            blk, off = divmod(c * LANES, bt)
            bias = bias_ref[kinds[blk], h, :, off:off + LANES]
            cols.append(s[:, cs] + bias + madd_ref[:, cs])
        mx = functools.reduce(jnp.maximum, cols)
        m_prev = m_ref[h]
        m_new = jnp.maximum(m_prev, jnp.max(mx, axis=1, keepdims=True))
        alpha = jnp.exp2(m_prev - m_new)
        p = jnp.concatenate([jnp.exp2(t - m_new) for t in cols], axis=1).astype(BF16)
        pv = jnp.dot(p, vaug_ref[:, 2 * h * hd:(2 * h + 2) * hd], preferred_element_type=F32)
        for half in range(2):
            a = slice((2 * h + half) * hd, (2 * h + half + 1) * hd)
            acc_ref[:, a] = alpha * acc_ref[:, a] + pv[:, half * hd:(half + 1) * hd]
        m_ref[h] = m_new

    @pl.when(last)
    def _():
        for h in range(DSA_HEADS):
            num = acc_ref[:, 2 * h * hd:(2 * h + 1) * hd]
            den = acc_ref[:, (2 * h + 1) * hd:(2 * h + 2) * hd]
            o_ref[:, h * hd:(h + 1) * hd] = (num / den).astype(o_ref.dtype)


def _attention(p2, mask, bias, q_col, k_col, v_col, batch, seq):
    m = p2.shape[0]
    width = DSA_HEADS * DSA_HEAD_DIM
    tq = _pick(seq, (256,))
    tk = _pick(seq, (512, 256))
    assert tq == BIAS_T and tk % BIAS_T == 0 and DSA_HEAD_DIM == LANES
    nq, nk = seq // tq, seq // tk
    assert q_col % width == 0 and k_col % width == 0 and v_col % width == 0
    qb, kb, vb = q_col // width, k_col // width, v_col // width
    pairs = [(i, j) for i in range(nq) for j in range((i * tq + tq - 1) // tk + 1)]
    qi = jnp.asarray([p[0] for p in pairs], jnp.int32)
    kj = jnp.asarray([p[1] for p in pairs], jnp.int32)
    grid_spec = pltpu.PrefetchScalarGridSpec(
        num_scalar_prefetch=2,
        grid=(batch, len(pairs)),
        in_specs=[pl.BlockSpec((tq, width), lambda b, p, qi, kj: (b * nq + qi[p], qb)),
                  pl.BlockSpec((tk, width), lambda b, p, qi, kj: (b * nk + kj[p], kb)),
                  pl.BlockSpec((tk, width), lambda b, p, qi, kj: (b * nk + kj[p], vb)),
                  pl.BlockSpec((None, tq, tk), lambda b, p, qi, kj: (b, qi[p], kj[p])),
                  pl.BlockSpec((3, DSA_HEADS, BIAS_T, BIAS_T), lambda b, p, qi, kj: (0, 0, 0, 0))],
        out_specs=pl.BlockSpec((tq, width), lambda b, p, qi, kj: (b * nq + qi[p], 0)),
        scratch_shapes=[pltpu.VMEM((DSA_HEADS, tq, LANES), F32),
                        pltpu.VMEM((tq, 2 * width), F32),
                        pltpu.VMEM((tk, 2 * width), BF16),
                        pltpu.VMEM((tq, tk), F32)])
    return pl.pallas_call(
        functools.partial(_attn_kernel, tq=tq, tk=tk),
        grid_spec=grid_spec,
        out_shape=jax.ShapeDtypeStruct((m, width), BF16),
        compiler_params=_params(("parallel", "arbitrary")),
        name="sparse_attn",
    )(qi, kj, p2, p2, p2, mask, bias)


def _split_w_in(w, d_model):
    gw = GDN_HEADS * GDN_HEAD_DIM
    dw = DSA_HEADS * DSA_HEAD_DIM
    iw = IDX_HEADS * IDX_HEAD_DIM
    sizes = (3 * gw, GDN_HEADS, GDN_HEADS, gw, dw, dw, dw, iw, IDX_HEAD_DIM, IDX_HEADS, d_model, d_model)
    offs = [0]
    for s in sizes:
        offs.append(offs[-1] + s)
    assert offs[-1] == w.shape[1]
    part = lambda i: w[:, offs[i]:offs[i + 1]]
    qkv_a, a_in, b_in, z, q_b, k_b, v_b, q_i, k_i, w_i, gate_a, gate_b = (part(i) for i in range(12))
    w_gdn = jnp.concatenate([qkv_a, z], axis=1).astype(BF16)
    fill = jnp.zeros((w.shape[0], LANES - 2 * GDN_HEADS - IDX_HEADS - IDX_HEAD_DIM), w.dtype)
    w_small = jnp.concatenate([a_in, b_in, w_i, fill, k_i], axis=1).astype(BF16)
    q_b = q_b * (DSA_HEAD_DIM ** -0.5 * LOG2E)
    w_rest = jnp.concatenate([q_b, k_b, v_b, q_i, gate_a, gate_b], axis=1).astype(BF16)
    cols = dict(q=0, k=dw, v=2 * dw, qi=3 * dw, ga=3 * dw + iw, gb=3 * dw + iw + d_model,
                w_lane=2 * GDN_HEADS, ki_lane=LANES - IDX_HEAD_DIM)
    return w_gdn, w_small, w_rest, cols


@jax.jit
def _forward(x, rel_bias, w_in, conv_w, a_log, dt_bias, gdn_norm_w, w_branch_a, w_branch_b, w_out,
             ln1_g, ln1_b, w_ffn_in, w_ffn_out, ln2_g, ln2_b):
    batch, seq, d_model = x.shape
    depth = w_in.shape[0]
    m = batch * seq
    alpha = (2 * depth) ** 0.25
    bias = _bias_tiles(rel_bias, BIAS_T)
    xf = x.reshape(m, d_model).astype(F32)
    xb = xf.astype(BF16)
    for l in range(depth):
        w_gdn, w_small, w_rest, cols = _split_w_in(w_in[l], d_model)
        p1 = _matmul(xb, w_gdn, F32, "proj_gdn")
        ps = _matmul(xb, w_small, F32, "proj_small")
        p2 = _matmul(xb, w_rest, BF16, "proj_rest")
        o_a = _gdn(p1, ps, conv_w[l], a_log[l], dt_bias[l], gdn_norm_w[l], batch, seq)
        ki = ps[:, cols["ki_lane"]:].astype(BF16).reshape(batch, seq, IDX_HEAD_DIM)
        mask = _indexer(p2, ps, ki, cols["qi"], cols["w_lane"], batch, seq)
        o_b = _attention(p2, mask, bias, cols["q"], cols["k"], cols["v"], batch, seq)
        merged = _merge(o_a, o_b, w_branch_a[l].astype(BF16), w_branch_b[l].astype(BF16), p2,
                        cols["ga"], cols["gb"], d_model)
        xf, xb = _mm_res_ln(merged, w_out[l].astype(BF16), xf, ln1_g[l], ln1_b[l], alpha, "out_proj_ln")
        act = _ffn_in(xb, w_ffn_in[l].astype(BF16))
        xf, xb = _mm_res_ln(act, w_ffn_out[l].astype(BF16), xf, ln2_g[l], ln2_b[l], alpha, "ffn_out_ln")
    return xf.reshape(batch, seq, d_model).astype(x.dtype)


def kernel(x, rel_bias, w_in, conv_w, a_log, dt_bias, gdn_norm_w, w_branch_a, w_branch_b, w_out, ln1_g, ln1_b,
           w_ffn_in, w_ffn_out, ln2_g, ln2_b):
    return _forward(x, rel_bias, w_in, conv_w, a_log, dt_bias, gdn_norm_w, w_branch_a, w_branch_b, w_out,
                    ln1_g, ln1_b, w_ffn_in, w_ffn_out, ln2_g, ln2_b)
```

```python
import functools
import math

import jax
import jax.numpy as jnp
from jax import lax
from jax.experimental import pallas as pl
from jax.experimental.pallas import tpu as pltpu

F32 = jnp.float32
BF16 = jnp.bfloat16

GDN_HEADS = 8
GDN_HEAD_DIM = 128
CONV_WIDTH = 4
CHUNK = 64
DSA_HEADS = 8
DSA_HEAD_DIM = 128
IDX_HEADS = 8
IDX_HEAD_DIM = 64
TOPK_MAX = 256
REL_BUCKETS = 32
REL_MAX_DIST = 128
LN_EPS = 1e-5
RMS_EPS = 1e-6

LANES = 128
VMEM_LIMIT = 56 * 1024 * 1024

NEG_BIG = -1e30
INT_MIN = -(2 ** 31)
LOG2E = math.log2(math.e)
BIAS_T = 256
GDN_HEADS_PER_STEP = 4


def _params(sem):
    return pltpu.CompilerParams(dimension_semantics=sem, vmem_limit_bytes=VMEM_LIMIT)


def _pick(n, prefs):
    for p in prefs:
        if n % p == 0:
            return p
    return n


def _mm_kernel(x_ref, w_ref, o_ref):
    o_ref[...] = jnp.dot(x_ref[...], w_ref[...], preferred_element_type=F32).astype(o_ref.dtype)


def _matmul(x, w, out_dtype, name):
    m, k = x.shape
    n = w.shape[1]
    tm = _pick(m, (1024, 512, 256, 128))
    tn = _pick(n, (512, 256, 128))
    return pl.pallas_call(
        _mm_kernel,
        grid=(m // tm, n // tn),
        in_specs=[pl.BlockSpec((tm, k), lambda i, j: (i, 0)),
                  pl.BlockSpec((k, tn), lambda i, j: (0, j))],
        out_specs=pl.BlockSpec((tm, tn), lambda i, j: (i, j)),
        out_shape=jax.ShapeDtypeStruct((m, n), out_dtype),
        compiler_params=_params(("parallel", "parallel")),
        name=name,
    )(x, w)


def _ffn_in_kernel(x_ref, wg_ref, wu_ref, o_ref):
    x = x_ref[...]
    g = jnp.dot(x, wg_ref[...], preferred_element_type=F32)
    u = jnp.dot(x, wu_ref[...], preferred_element_type=F32)
    o_ref[...] = (g * jax.nn.sigmoid(g) * u).astype(o_ref.dtype)


def _ffn_in(x, w):
    m, k = x.shape
    f = w.shape[1] // 2
    tm = _pick(m, (1024, 512, 256, 128))
    tn = _pick(f, (512, 256, 128))
    nb = f // tn
    return pl.pallas_call(
        _ffn_in_kernel,
        grid=(m // tm, nb),
        in_specs=[pl.BlockSpec((tm, k), lambda i, j: (i, 0)),
                  pl.BlockSpec((k, tn), lambda i, j: (0, j)),
                  pl.BlockSpec((k, tn), lambda i, j: (0, j + nb))],
        out_specs=pl.BlockSpec((tm, tn), lambda i, j: (i, j)),
        out_shape=jax.ShapeDtypeStruct((m, f), BF16),
        compiler_params=_params(("parallel", "parallel")),
        name="ffn_in",
    )(x, w, w)


def _mm_res_ln_kernel(a_ref, w_ref, x_ref, g_ref, b_ref, of_ref, ob_ref, acc_ref, *, nk, alpha):
    k = pl.program_id(1)

    @pl.when(k == 0)
    def _():
        acc_ref[...] = jnp.zeros_like(acc_ref)

    acc_ref[...] += jnp.dot(a_ref[...], w_ref[...], preferred_element_type=F32)

    @pl.when(k == nk - 1)
    def _():
        y = alpha * x_ref[...] + acc_ref[...]
        mu = jnp.mean(y, axis=-1, keepdims=True)
        yc = y - mu
        var = jnp.mean(yc * yc, axis=-1, keepdims=True)
        out = yc * lax.rsqrt(var + LN_EPS) * g_ref[...] + b_ref[...]
        of_ref[...] = out
        ob_ref[...] = out.astype(BF16)


def _mm_res_ln(a, w, x, g, b, alpha, name):
    m, kdim = a.shape
    n = w.shape[1]
    tm = _pick(m, (512, 256, 128))
    tk = _pick(kdim, (1408, 1024, 512, 256, 128))
    nk = kdim // tk
    return pl.pallas_call(
        functools.partial(_mm_res_ln_kernel, nk=nk, alpha=alpha),
        grid=(m // tm, nk),
        in_specs=[pl.BlockSpec((tm, tk), lambda i, k: (i, k)),
                  pl.BlockSpec((tk, n), lambda i, k: (k, 0)),
                  pl.BlockSpec((tm, n), lambda i, k: (i, 0)),
                  pl.BlockSpec((1, n), lambda i, k: (0, 0)),
                  pl.BlockSpec((1, n), lambda i, k: (0, 0))],
        out_specs=[pl.BlockSpec((tm, n), lambda i, k: (i, 0)),
                   pl.BlockSpec((tm, n), lambda i, k: (i, 0))],
        out_shape=[jax.ShapeDtypeStruct((m, n), F32), jax.ShapeDtypeStruct((m, n), BF16)],
        scratch_shapes=[pltpu.VMEM((tm, n), F32)],
        compiler_params=_params(("parallel", "arbitrary")),
        name=name,
    )(a, w, x, g.reshape(1, n), b.reshape(1, n))


def _merge_kernel(oa_ref, ob_ref, wa_ref, wb_ref, ga_ref, gb_ref, o_ref):
    a = jnp.dot(oa_ref[...], wa_ref[...], preferred_element_type=F32)
    b = jnp.dot(ob_ref[...], wb_ref[...], preferred_element_type=F32)
    ga = jax.nn.sigmoid(ga_ref[...].astype(F32))
    gb = jax.nn.sigmoid(gb_ref[...].astype(F32))
    o_ref[...] = (ga * a + gb * b).astype(o_ref.dtype)


def _merge(o_a, o_b, w_a, w_b, p2, ga_col, gb_col, d_model):
    m, ka = o_a.shape
    kb = o_b.shape[1]
    tm = _pick(m, (1024, 512, 256, 128))
    tn = 512
    assert d_model % tn == 0 and ga_col % tn == 0 and gb_col % tn == 0
    ga_blk, gb_blk = ga_col // tn, gb_col // tn
    return pl.pallas_call(
        _merge_kernel,
        grid=(m // tm, d_model // tn),
        in_specs=[pl.BlockSpec((tm, ka), lambda i, j: (i, 0)),
                  pl.BlockSpec((tm, kb), lambda i, j: (i, 0)),
                  pl.BlockSpec((ka, tn), lambda i, j: (0, j)),
                  pl.BlockSpec((kb, tn), lambda i, j: (0, j)),
                  pl.BlockSpec((tm, tn), lambda i, j: (i, ga_blk + j)),
                  pl.BlockSpec((tm, tn), lambda i, j: (i, gb_blk + j))],
        out_specs=pl.BlockSpec((tm, tn), lambda i, j: (i, j)),
        out_shape=jax.ShapeDtypeStruct((m, d_model), BF16),
        compiler_params=_params(("parallel", "parallel")),
        name="merge",
    )(o_a, o_b, w_a, w_b, p2, p2)


def _softplus(x):
    return jnp.maximum(x, 0.0) + jnp.log1p(jnp.exp(-jnp.abs(x)))


def _gdn_kernel(q_ref, k_ref, v_ref, z_ref, sm_ref, cwq_ref, cwk_ref, cwv_ref, alog_ref, dtb_ref, nw_ref,
                o_ref, ext_ref, halo_ref, state_ref, u_ref, w_ref, qd_ref, kdt_ref, in_ref, gt_ref, *, nc, hp):
    h0 = pl.program_id(1) * hp
    tb = pl.program_id(2)
    t = nc * CHUNK
    dk = GDN_HEAD_DIM

    @pl.when(tb == 0)
    def _():
        halo_ref[...] = jnp.zeros_like(halo_ref)
        state_ref[...] = jnp.zeros_like(state_ref)

    def conv_silu(x_ref, cw_ref, slot):
        x = x_ref[...]
        ext_ref[0:8, :] = halo_ref[slot]
        ext_ref[8:8 + t, :] = x
        halo_ref[slot] = x[t - 8:t, :]
        cw = cw_ref[...]
        y = cw[CONV_WIDTH - 1:CONV_WIDTH, :] * x
        for j in range(CONV_WIDTH - 1):
            y = y + cw[j:j + 1, :] * ext_ref[pl.ds(8 - (CONV_WIDTH - 1) + j, t), :]
        return y * jax.nn.sigmoid(y)

    q = conv_silu(q_ref, cwq_ref, 0)
    k = conv_silu(k_ref, cwk_ref, 1)
    v = conv_silu(v_ref, cwv_ref, 2)

    sm = sm_ref[...]
    lane = lax.broadcasted_iota(jnp.int32, sm.shape, 1)
    g_all = -jnp.exp(alog_ref[...]) * _softplus(sm + dtb_ref[...])
    beta_all = jax.nn.sigmoid(sm)

    c3 = lambda a: a.reshape(nc, CHUNK, a.shape[-1])
    ri = lax.broadcasted_iota(jnp.int32, (CHUNK, CHUNK), 0)
    ci = lax.broadcasted_iota(jnp.int32, (CHUNK, CHUNK), 1)
    causal = (ci <= ri)[None]
    strict = (ci < ri)[None]
    eye = (ci == ri).astype(F32)[None]
    upper = (ri <= ci).astype(F32)[None]
    bmm = functools.partial(jnp.einsum, preferred_element_type=F32)

    for e in range(hp):
        hs = slice(e * dk, (e + 1) * dk)
        g_col = jnp.sum(jnp.where(lane == h0 + e, g_all, 0.0), axis=1, keepdims=True)
        beta = jnp.sum(jnp.where(lane == GDN_HEADS + h0 + e, beta_all, 0.0), axis=1, keepdims=True)
        qe, ke, ve = q[:, hs], k[:, hs], v[:, hs]
        qn = qe * lax.rsqrt(jnp.sum(qe * qe, axis=1, keepdims=True) + RMS_EPS) * (dk ** -0.5)
        kn = ke * lax.rsqrt(jnp.sum(ke * ke, axis=1, keepdims=True) + RMS_EPS)
        kb = kn * beta
        vb = ve * beta

        g3 = c3(g_col)
        gc_row = jnp.sum(g3 * upper, axis=1, keepdims=True)
        gc_col = jnp.sum(eye * gc_row, axis=2, keepdims=True)
        gc_last = gc_row[:, :, CHUNK - 1:CHUNK]
        decay = jnp.exp(jnp.where(causal, gc_col - gc_row, NEG_BIG))
        e_col = jnp.exp(gc_col)

        qn3, kn3, kb3, vb3 = c3(qn), c3(kn), c3(kb), c3(vb)
        kn3b = kn3.astype(BF16)
        kk = bmm("cid,cjd->cij", kb3.astype(BF16), kn3b)
        mneg = jnp.where(strict, -(kk * decay), 0.0)
        inv = eye + mneg
        pw = mneg.astype(BF16)
        for _ in range(int(math.log2(CHUNK)) - 1):
            pw = bmm("cij,cjk->cik", pw, pw).astype(BF16)
            inv = inv + bmm("cij,cjk->cik", inv.astype(BF16), pw)
        invb = inv.astype(BF16)
        u_ref[e] = bmm("cij,cjd->cid", invb, vb3.astype(BF16))
        w_ref[e] = bmm("cij,cjd->cid", invb, (kb3 * e_col).astype(BF16))
        in_ref[e] = bmm("cid,cjd->cij", qn3.astype(BF16), kn3b) * decay
        qd_ref[e] = qn3 * e_col
        kdt_ref[e] = jnp.swapaxes(kn3 * jnp.exp(gc_last - gc_col), 1, 2)
        gt_ref[e] = jnp.exp(gc_last)

    nw = nw_ref[...]
    for c in range(nc):
        rows = slice(c * CHUNK, (c + 1) * CHUNK)
        for e in range(hp):
            hs = slice(e * dk, (e + 1) * dk)
            s = state_ref[e]
            sb = s.astype(BF16)
            v_new = u_ref[e, c] - jnp.dot(w_ref[e, c].astype(BF16), sb, preferred_element_type=F32)
            vnb = v_new.astype(BF16)
            o = (jnp.dot(qd_ref[e, c].astype(BF16), sb, preferred_element_type=F32)
                 + jnp.dot(in_ref[e, c].astype(BF16), vnb, preferred_element_type=F32))
            state_ref[e] = s * gt_ref[e, c] + jnp.dot(kdt_ref[e, c].astype(BF16), vnb,
                                                      preferred_element_type=F32)
            zc = z_ref[rows, hs]
            o = o * lax.rsqrt(jnp.mean(o * o, axis=1, keepdims=True) + RMS_EPS) * nw * (zc * jax.nn.sigmoid(zc))
            o_ref[rows, hs] = o.astype(o_ref.dtype)


def _gdn(p1, ps, conv_w, a_log, dt_bias, norm_w, batch, seq):
    m = p1.shape[0]
    hd = GDN_HEAD_DIM
    nh = GDN_HEADS
    t = _pick(seq, (512, 256, 128, 64))
    nc = t // CHUNK
    nt = seq // t
    hp = GDN_HEADS_PER_STEP
    ng = nh // hp
    wd = hp * hd
    pad = lambda a: jnp.pad(a.astype(F32), (0, LANES - a.shape[0])).reshape(1, LANES)
    row = lambda b, h, i: b * nt + i
    col_spec = lambda off: pl.BlockSpec((t, wd), lambda b, h, i: (row(b, h, i), off + h))
    cw_spec = lambda off: pl.BlockSpec((CONV_WIDTH, wd), lambda b, h, i: (0, off + h))
    vec_spec = pl.BlockSpec((1, LANES), lambda b, h, i: (0, 0))
    return pl.pallas_call(
        functools.partial(_gdn_kernel, nc=nc, hp=hp),
        grid=(batch, ng, nt),
        in_specs=[col_spec(0), col_spec(ng), col_spec(2 * ng), col_spec(3 * ng),
                  pl.BlockSpec((t, LANES), lambda b, h, i: (row(b, h, i), 0)),
                  cw_spec(0), cw_spec(ng), cw_spec(2 * ng),
                  vec_spec, vec_spec, vec_spec],
        out_specs=pl.BlockSpec((t, wd), lambda b, h, i: (row(b, h, i), h)),
        out_shape=jax.ShapeDtypeStruct((m, nh * hd), BF16),
        scratch_shapes=[pltpu.VMEM((t + 8, wd), F32),
                        pltpu.VMEM((3, 8, wd), F32),
                        pltpu.VMEM((hp, hd, hd), F32),
                        pltpu.VMEM((hp, nc, CHUNK, hd), F32),
                        pltpu.VMEM((hp, nc, CHUNK, hd), F32),
                        pltpu.VMEM((hp, nc, CHUNK, hd), F32),
                        pltpu.VMEM((hp, nc, hd, CHUNK), F32),
                        pltpu.VMEM((hp, nc, CHUNK, CHUNK), F32),
                        pltpu.VMEM((hp, nc, 1, 1), F32)],
        compiler_params=_params(("parallel", "parallel", "arbitrary")),
        name="gdn",
    )(p1, p1, p1, p1, ps, conv_w, conv_w, conv_w, pad(a_log), pad(dt_bias), norm_w.reshape(1, hd).astype(F32))


def _key_to_float(key):
    bits = jnp.where(key >= 0, key, key ^ jnp.int32(0x7FFFFFFF))
    return lax.bitcast_convert_type(bits, F32)


def _upper_half(x):
    bits = lax.bitcast_convert_type(x, jnp.int32) & jnp.int32(-65536)
    return lax.bitcast_convert_type(bits, F32).astype(BF16)


def _indexer_kernel(qi_ref, ws_ref, ki_ref, mask_ref, sc_ref, sch_ref, *, tq, tk, seq, topk, w_lane):
    i = pl.program_id(1)
    t0 = i * tq
    nt = (t0 + tq + tk - 1) // tk
    n_tiles = seq // tk
    sl = 8
    ways = 4
    key0 = lax.broadcasted_iota(jnp.int32, (tk, tq), 0)
    qpos = t0 + lax.broadcasted_iota(jnp.int32, (tk, tq), 1)

    qi = qi_ref[...]
    ws_t = jnp.transpose(ws_ref[...])
    q_heads = [qi[:, h * IDX_HEAD_DIM:(h + 1) * IDX_HEAD_DIM] for h in range(IDX_HEADS)]
    w_rows = [ws_t[w_lane + h:w_lane + h + 1, :] for h in range(IDX_HEADS)]

    def score_tile(j, carry):
        s0 = pl.multiple_of(j * tk, tk)
        kt = ki_ref[pl.ds(s0, tk), :]
        acc = jnp.zeros((tk, tq), F32)
        for h in range(IDX_HEADS):
            r = lax.dot_general(kt, q_heads[h], (((1,), (1,)), ((), ())), preferred_element_type=F32)
            acc = acc + w_rows[h] * jnp.maximum(r, 0.0)
        sc = jnp.where(key0 + s0 <= qpos, acc, -jnp.inf)
        sc_ref[pl.ds(s0, tk), :] = sc
        sch_ref[pl.ds(s0, tk), :] = _upper_half(sc)
        return carry

    lax.fori_loop(0, nt, score_tile, 0)

    def count_upper(cf):
        ch = jnp.broadcast_to(_upper_half(cf), (tk, tq))
        pk = 2 * sl

        def body(j, acc):
            s0 = pl.multiple_of(j * tk, tk)
            ind = jnp.where(sch_ref[pl.ds(s0, tk), :] >= ch, jnp.bfloat16(1), jnp.bfloat16(0))
            parts = ind.reshape(tk // (pk * ways), ways, pk, tq)
            for r in range(parts.shape[0]):
                acc = acc + parts[r]
            return acc
        assert (seq // (pk * ways)) < 256
        acc = lax.fori_loop(0, nt, body, jnp.zeros((ways, pk, tq), BF16))
        return jnp.sum(jnp.sum(acc.astype(F32), axis=0), axis=0, keepdims=True)

    def count(pred):
        def body(j, acc):
            s0 = pl.multiple_of(j * tk, tk)
            ind = jnp.where(pred(sc_ref[pl.ds(s0, tk), :]), 1.0, 0.0)
            return acc + jnp.sum(ind.reshape(tk // (sl * ways), ways, sl, tq), axis=0)
        acc = lax.fori_loop(0, nt, body, jnp.zeros((ways, sl, tq), F32))
        return jnp.sum(jnp.sum(acc, axis=0), axis=0, keepdims=True)

    kf = jnp.float32(topk)
    rows = lambda a: jnp.broadcast_to(a, (tk, tq))
    c_nonneg = count_upper(jnp.zeros((1, tq), F32))
    nonneg = c_nonneg >= kf
    base0 = jnp.where(nonneg, jnp.int32(0), jnp.int32(INT_MIN))
    cnt0 = jnp.where(nonneg, c_nonneg, jnp.float32(seq + 1))

    def bit_step(count_ge):
        def step(b, carry):
            base, cnt = carry
            cand = base | jnp.left_shift(jnp.int32(1), 30 - b)
            c = count_ge(_key_to_float(cand))
            ok = c >= kf
            return jnp.where(ok, cand, base), jnp.where(ok, c, cnt)
        return step

    carry = lax.fori_loop(0, 15, bit_step(count_upper), (base0, cnt0))
    base, n_ge = lax.fori_loop(15, 31, bit_step(lambda cf: count(lambda x: x >= rows(cf))), carry)
    no_thr = base == jnp.int32(INT_MIN)
    take_all = rows(no_thr)
    thr = rows(_key_to_float(base))
    tie_excess = jnp.logical_and(jnp.logical_not(no_thr), n_ge > kf)
    any_excess = jnp.max(jnp.where(tie_excess, 1.0, 0.0)) > 0.0

    def write(j, sel):
        s0 = pl.multiple_of(j * tk, tk)
        keep = jnp.logical_and(key0 + s0 <= qpos, jnp.logical_or(take_all, sel))
        keep_qk = jnp.transpose(jnp.where(keep, 1.0, 0.0))
        mask_ref[:, pl.ds(s0, tk)] = jnp.where(keep_qk > 0.5, 1, 0).astype(jnp.int8)

    @pl.when(jnp.logical_not(any_excess))
    def _():
        def body(j, carry):
            s0 = pl.multiple_of(j * tk, tk)
            write(j, sc_ref[pl.ds(s0, tk), :] >= thr)
            return carry
        lax.fori_loop(0, nt, body, 0)

    @pl.when(any_excess)
    def _():
        need = rows(kf - count(lambda x: x > thr))
        earlier = (lax.broadcasted_iota(jnp.int32, (tk, tk), 1)
                   < lax.broadcasted_iota(jnp.int32, (tk, tk), 0)).astype(BF16)

        def body(j, seen):
            s0 = pl.multiple_of(j * tk, tk)
            x = sc_ref[pl.ds(s0, tk), :]
            eq = x == thr
            eqf = jnp.where(eq, 1.0, 0.0)
            rank = rows(seen) + jnp.dot(earlier, eqf.astype(BF16), preferred_element_type=F32)
            write(j, jnp.logical_or(x > thr, jnp.logical_and(eq, rank < need)))
            return seen + jnp.sum(eqf, axis=0, keepdims=True)
        lax.fori_loop(0, nt, body, jnp.zeros((1, tq), F32))

    def clear(j, carry):
        s0 = pl.multiple_of(j * tk, tk)
        mask_ref[:, pl.ds(s0, tk)] = jnp.zeros((tq, tk), jnp.int8)
        return carry

    lax.fori_loop(nt, n_tiles, clear, 0)


def _indexer(p2, ps, ki, qi_col, w_lane, batch, seq):
    tq = _pick(seq, (256, 128))
    tk = _pick(seq, (512, 256, 128))
    nq = seq // tq
    qw = IDX_HEADS * IDX_HEAD_DIM
    assert qi_col % qw == 0
    topk = min(TOPK_MAX, seq // 4)
    return pl.pallas_call(
        functools.partial(_indexer_kernel, tq=tq, tk=tk, seq=seq, topk=topk, w_lane=w_lane),
        grid=(batch, nq),
        in_specs=[pl.BlockSpec((tq, qw), lambda b, i: (b * nq + i, qi_col // qw)),
                  pl.BlockSpec((tq, LANES), lambda b, i: (b * nq + i, 0)),
                  pl.BlockSpec((None, seq, IDX_HEAD_DIM), lambda b, i: (b, 0, 0))],
        out_specs=pl.BlockSpec((None, tq, seq), lambda b, i: (b, i, 0)),
        out_shape=jax.ShapeDtypeStruct((batch, seq, seq), jnp.int8),
        scratch_shapes=[pltpu.VMEM((seq, tq), F32),
                        pltpu.VMEM((seq, tq), BF16)],
        compiler_params=_params(("parallel", "parallel")),
        name="indexer",
    )(p2, ps, ki)


def _bias_kernel(rb_ref, o_ref, *, t):
    d = pl.program_id(0)
    h = pl.program_id(1)
    r = lax.broadcasted_iota(jnp.int32, (t, t), 0)
    c = lax.broadcasted_iota(jnp.int32, (t, t), 1)
    n = jnp.maximum(r - c + d * t, 0)
    max_exact = REL_BUCKETS // 2
    log_ratio = jnp.log(jnp.maximum(n, 1).astype(F32) / max_exact) / math.log(REL_MAX_DIST / max_exact)
    large = max_exact + (log_ratio * (REL_BUCKETS - max_exact)).astype(jnp.int32)
    large = jnp.minimum(large, REL_BUCKETS - 1)
    bucket = jnp.where(n < max_exact, n, large)
    out = jnp.zeros((t, t), F32)
    for b in range(REL_BUCKETS):
        out = jnp.where(bucket == b, rb_ref[b, h], out)
    o_ref[...] = out * LOG2E


def _bias_tiles(rel_bias, t):
    assert t >= REL_MAX_DIST
    return pl.pallas_call(
        functools.partial(_bias_kernel, t=t),
        grid=(3, DSA_HEADS),
        in_specs=[pl.BlockSpec(memory_space=pltpu.SMEM)],
        out_specs=pl.BlockSpec((None, None, t, t), lambda d, h: (d, h, 0, 0)),
        out_shape=jax.ShapeDtypeStruct((3, DSA_HEADS, t, t), F32),
        compiler_params=_params(("parallel", "parallel")),
        name="rel_bias_tiles",
    )(rel_bias.astype(F32))


def _attn_kernel(qi_ref, kj_ref, q_ref, k_ref, v_ref, mask_ref, bias_ref, o_ref,
                 m_ref, acc_ref, vaug_ref, madd_ref, *, tq, tk):
    pair = pl.program_id(1)
    i = qi_ref[pair]
    j = kj_ref[pair]
    hd = DSA_HEAD_DIM
    bt = BIAS_T
    first = j == 0
    last = j == (i * tq + tq - 1) // tk

    @pl.when(pair == 0)
    def _():
        vaug_ref[...] = jnp.ones_like(vaug_ref)

    @pl.when(first)
    def _():
        m_ref[...] = jnp.full_like(m_ref, NEG_BIG)
        acc_ref[...] = jnp.zeros_like(acc_ref)

    def stage():
        madd_ref[...] = jnp.where(mask_ref[...].astype(jnp.int32) != 0, 0.0, NEG_BIG)
        for h in range(DSA_HEADS):
            vaug_ref[:, 2 * h * hd:(2 * h + 1) * hd] = v_ref[:, h * hd:(h + 1) * hd]

    lead = (i * tq - j * tk) // bt
    kinds = [jnp.clip(lead - c, 0, 2) for c in range(tk // bt)]
    far = lead - (tk // bt - 1) >= 2

    def head_step(h, far_tile):
        sl = slice(h * hd, (h + 1) * hd)
        s = lax.dot_general(q_ref[:, sl], k_ref[:, sl], (((1,), (1,)), ((), ())),
                            preferred_element_type=F32)
        cols = []
        for c in range(tk // LANES):
            cs = slice(c * LANES, (c + 1) * LANES)
            blk, off = divmod(c * LANES, bt)
            t = s[:, cs] + madd_ref[:, cs]
            cols.append(t if far_tile else t + bias_ref[kinds[blk], h, :, off:off + LANES])
        mx = jnp.max(functools.reduce(jnp.maximum, cols), axis=1, keepdims=True)
        m_prev = m_ref[h]
        if far_tile:
            cb = bias_ref[2, h, :, 0:LANES]
            m_new = jnp.maximum(m_prev, mx + cb)
            shift = m_new - cb
        else:
            m_new = jnp.maximum(m_prev, mx)
            shift = m_new
        alpha = jnp.exp2(m_prev - m_new)
        p = jnp.concatenate([jnp.exp2(t - shift) for t in cols], axis=1).astype(BF16)
        pv = jnp.dot(p, vaug_ref[:, 2 * h * hd:(2 * h + 2) * hd], preferred_element_type=F32)
        for half in range(2):
            a = slice((2 * h + half) * hd, (2 * h + half + 1) * hd)
            acc_ref[:, a] = alpha * acc_ref[:, a] + pv[:, half * hd:(half + 1) * hd]
        m_ref[h] = m_new

    @pl.when(far)
    def _():
        stage()
        for h in range(DSA_HEADS):
            head_step(h, True)

    @pl.when(jnp.logical_not(far))
    def _():
        stage()
        for h in range(DSA_HEADS):
            head_step(h, False)

    @pl.when(last)
    def _():
        for h in range(DSA_HEADS):
            num = acc_ref[:, 2 * h * hd:(2 * h + 1) * hd]
            den = acc_ref[:, (2 * h + 1) * hd:(2 * h + 2) * hd]
            o_ref[:, h * hd:(h + 1) * hd] = (num / den).astype(o_ref.dtype)


def _attention(p2, mask, bias, q_col, k_col, v_col, batch, seq):
    m = p2.shape[0]
    width = DSA_HEADS * DSA_HEAD_DIM
    tq = _pick(seq, (256,))
    tk = _pick(seq, (512, 256))
    assert tq == BIAS_T and tk % BIAS_T == 0 and DSA_HEAD_DIM == LANES
    nq, nk = seq // tq, seq // tk
    assert q_col % width == 0 and k_col % width == 0 and v_col % width == 0
    qb, kb, vb = q_col // width, k_col // width, v_col // width
    pairs = [(i, j) for i in range(nq) for j in range((i * tq + tq - 1) // tk + 1)]
    qi = jnp.asarray([p[0] for p in pairs], jnp.int32)
    kj = jnp.asarray([p[1] for p in pairs], jnp.int32)
    grid_spec = pltpu.PrefetchScalarGridSpec(
        num_scalar_prefetch=2,
        grid=(batch, len(pairs)),
        in_specs=[pl.BlockSpec((tq, width), lambda b, p, qi, kj: (b * nq + qi[p], qb)),
                  pl.BlockSpec((tk, width), lambda b, p, qi, kj: (b * nk + kj[p], kb)),
                  pl.BlockSpec((tk, width), lambda b, p, qi, kj: (b * nk + kj[p], vb)),
                  pl.BlockSpec((None, tq, tk), lambda b, p, qi, kj: (b, qi[p], kj[p])),
                  pl.BlockSpec((3, DSA_HEADS, BIAS_T, BIAS_T), lambda b, p, qi, kj: (0, 0, 0, 0))],
        out_specs=pl.BlockSpec((tq, width), lambda b, p, qi, kj: (b * nq + qi[p], 0)),
        scratch_shapes=[pltpu.VMEM((DSA_HEADS, tq, LANES), F32),
                        pltpu.VMEM((tq, 2 * width), F32),
                        pltpu.VMEM((tk, 2 * width), BF16),
                        pltpu.VMEM((tq, tk), F32)])
    return pl.pallas_call(
        functools.partial(_attn_kernel, tq=tq, tk=tk),
        grid_spec=grid_spec,
        out_shape=jax.ShapeDtypeStruct((m, width), BF16),
        compiler_params=_params(("parallel", "arbitrary")),
        name="sparse_attn",
    )(qi, kj, p2, p2, p2, mask, bias)


def _cast_kernel(x_ref, o_ref):
    o_ref[...] = x_ref[...].astype(o_ref.dtype)


def _to_bf16(w, layer, name):
    _, r, c = w.shape
    rb = next(b for b in (1024, 512, 256, 128, 64, 32, 16) if r % b == 0 and b * c * 4 <= 8 * 2 ** 20)
    return pl.pallas_call(
        _cast_kernel,
        grid=(r // rb,),
        in_specs=[pl.BlockSpec((None, rb, c), lambda i: (layer, i, 0))],
        out_specs=pl.BlockSpec((rb, c), lambda i: (i, 0)),
        out_shape=jax.ShapeDtypeStruct((r, c), BF16),
        compiler_params=_params(("parallel",)),
        name=name,
    )(w)


def _regroup_kernel(w_ref, og_ref, os_ref, or_ref, *, offs, q_scale):
    x = w_ref[...]
    part = lambda i: x[:, offs[i]:offs[i + 1]]
    qkv_a, a_in, b_in, z, q_b, k_b, v_b, q_i, k_i, w_i, gate_a, gate_b = (part(i) for i in range(12))
    fill = jnp.zeros((x.shape[0], LANES - 2 * GDN_HEADS - IDX_HEADS - IDX_HEAD_DIM), x.dtype)
    og_ref[...] = jnp.concatenate([qkv_a, z], axis=1).astype(BF16)
    os_ref[...] = jnp.concatenate([a_in, b_in, w_i, fill, k_i], axis=1).astype(BF16)
    or_ref[...] = jnp.concatenate([q_b * q_scale, k_b, v_b, q_i, gate_a, gate_b], axis=1).astype(BF16)


def _split_w_in(w, layer, d_model):
    gw = GDN_HEADS * GDN_HEAD_DIM
    dw = DSA_HEADS * DSA_HEAD_DIM
    iw = IDX_HEADS * IDX_HEAD_DIM
    sizes = (3 * gw, GDN_HEADS, GDN_HEADS, gw, dw, dw, dw, iw, IDX_HEAD_DIM, IDX_HEADS, d_model, d_model)
    offs = [0]
    for s in sizes:
        offs.append(offs[-1] + s)
    _, r, d_in = w.shape
    assert offs[-1] == d_in
    rb = _pick(r, (128, 64, 32, 16))
    n_gdn, n_rest = 4 * gw, 3 * dw + iw + 2 * d_model
    w_gdn, w_small, w_rest = pl.pallas_call(
        functools.partial(_regroup_kernel, offs=tuple(offs), q_scale=DSA_HEAD_DIM ** -0.5 * LOG2E),
        grid=(r // rb,),
        in_specs=[pl.BlockSpec((None, rb, d_in), lambda i: (layer, i, 0))],
        out_specs=[pl.BlockSpec((rb, n_gdn), lambda i: (i, 0)),
                   pl.BlockSpec((rb, LANES), lambda i: (i, 0)),
                   pl.BlockSpec((rb, n_rest), lambda i: (i, 0))],
        out_shape=[jax.ShapeDtypeStruct((r, n_gdn), BF16),
                   jax.ShapeDtypeStruct((r, LANES), BF16),
                   jax.ShapeDtypeStruct((r, n_rest), BF16)],
        compiler_params=_params(("parallel",)),
        name="regroup_w_in",
    )(w)
    cols = dict(q=0, k=dw, v=2 * dw, qi=3 * dw, ga=3 * dw + iw, gb=3 * dw + iw + d_model,
                w_lane=2 * GDN_HEADS, ki_lane=LANES - IDX_HEAD_DIM)
    return w_gdn, w_small, w_rest, cols


@jax.jit
def _forward(x, rel_bias, w_in, conv_w, a_log, dt_bias, gdn_norm_w, w_branch_a, w_branch_b, w_out,
             ln1_g, ln1_b, w_ffn_in, w_ffn_out, ln2_g, ln2_b):
    batch, seq, d_model = x.shape
    depth = w_in.shape[0]
    m = batch * seq
    alpha = (2 * depth) ** 0.25
    bias = _bias_tiles(rel_bias, BIAS_T)
    xf = x.reshape(m, d_model).astype(F32)
    xb = _to_bf16(xf.reshape(1, m, d_model), 0, "cast_x")
    for l in range(depth):
        w_gdn, w_small, w_rest, cols = _split_w_in(w_in, l, d_model)
        p1 = _matmul(xb, w_gdn, F32, "proj_gdn")
        ps = _matmul(xb, w_small, F32, "proj_small")
        p2 = _matmul(xb, w_rest, BF16, "proj_rest")
        o_a = _gdn(p1, ps, conv_w[l], a_log[l], dt_bias[l], gdn_norm_w[l], batch, seq)
        ki = ps[:, cols["ki_lane"]:].astype(BF16).reshape(batch, seq, IDX_HEAD_DIM)
        mask = _indexer(p2, ps, ki, cols["qi"], cols["w_lane"], batch, seq)
        o_b = _attention(p2, mask, bias, cols["q"], cols["k"], cols["v"], batch, seq)
        merged = _merge(o_a, o_b, _to_bf16(w_branch_a, l, "cast_w_a"), _to_bf16(w_branch_b, l, "cast_w_b"), p2,
                        cols["ga"], cols["gb"], d_model)
        xf, xb = _mm_res_ln(merged, _to_bf16(w_out, l, "cast_w_out"), xf, ln1_g[l], ln1_b[l], alpha,
                            "out_proj_ln")
        act = _ffn_in(xb, _to_bf16(w_ffn_in, l, "cast_w_ffn_in"))
        xf, xb = _mm_res_ln(act, _to_bf16(w_ffn_out, l, "cast_w_ffn_out"), xf, ln2_g[l], ln2_b[l], alpha,
                            "ffn_out_ln")
    return xf.reshape(batch, seq, d_model).astype(x.dtype)


def kernel(x, rel_bias, w_in, conv_w, a_log, dt_bias, gdn_norm_w, w_branch_a, w_branch_b, w_out, ln1_g, ln1_b,
           w_ffn_in, w_ffn_out, ln2_g, ln2_b):
    return _forward(x, rel_bias, w_in, conv_w, a_log, dt_bias, gdn_norm_w, w_branch_a, w_branch_b, w_out,
                    ln1_g, ln1_b, w_ffn_in, w_ffn_out, ln2_g, ln2_b)
```

```python
import functools
import math

import jax
import jax.numpy as jnp
from jax import lax
from jax.experimental import pallas as pl
from jax.experimental.pallas import tpu as pltpu

F32 = jnp.float32
BF16 = jnp.bfloat16

GDN_HEADS = 8
GDN_HEAD_DIM = 128
CONV_WIDTH = 4
CHUNK = 64
DSA_HEADS = 8
DSA_HEAD_DIM = 128
IDX_HEADS = 8
IDX_HEAD_DIM = 64
TOPK_MAX = 256
REL_BUCKETS = 32
REL_MAX_DIST = 128
LN_EPS = 1e-5
RMS_EPS = 1e-6

LANES = 128
VMEM_LIMIT = 56 * 1024 * 1024

NEG_BIG = -1e30
INT_MIN = -(2 ** 31)
LOG2E = math.log2(math.e)
BIAS_T = 256
GDN_HEADS_PER_STEP = 4


def _params(sem):
    return pltpu.CompilerParams(dimension_semantics=sem, vmem_limit_bytes=VMEM_LIMIT)


def _pick(n, prefs):
    for p in prefs:
        if n % p == 0:
            return p
    return n


def _mm_kernel(x_ref, w_ref, o_ref):
    o_ref[...] = jnp.dot(x_ref[...], w_ref[...], preferred_element_type=F32).astype(o_ref.dtype)


def _matmul(x, w, out_dtype, name):
    m, k = x.shape
    n = w.shape[1]
    tm = _pick(m, (1024, 512, 256, 128))
    tn = _pick(n, (512, 256, 128))
    return pl.pallas_call(
        _mm_kernel,
        grid=(m // tm, n // tn),
        in_specs=[pl.BlockSpec((tm, k), lambda i, j: (i, 0)),
                  pl.BlockSpec((k, tn), lambda i, j: (0, j))],
        out_specs=pl.BlockSpec((tm, tn), lambda i, j: (i, j)),
        out_shape=jax.ShapeDtypeStruct((m, n), out_dtype),
        compiler_params=_params(("parallel", "parallel")),
        name=name,
    )(x, w)


def _ffn_in_kernel(x_ref, wg_ref, wu_ref, o_ref):
    x = x_ref[...]
    g = jnp.dot(x, wg_ref[...], preferred_element_type=F32)
    u = jnp.dot(x, wu_ref[...], preferred_element_type=F32)
    o_ref[...] = (g * jax.nn.sigmoid(g) * u).astype(o_ref.dtype)


def _ffn_in(x, w):
    m, k = x.shape
    f = w.shape[1] // 2
    tm = _pick(m, (1024, 512, 256, 128))
    tn = _pick(f, (512, 256, 128))
    nb = f // tn
    return pl.pallas_call(
        _ffn_in_kernel,
        grid=(m // tm, nb),
        in_specs=[pl.BlockSpec((tm, k), lambda i, j: (i, 0)),
                  pl.BlockSpec((k, tn), lambda i, j: (0, j)),
                  pl.BlockSpec((k, tn), lambda i, j: (0, j + nb))],
        out_specs=pl.BlockSpec((tm, tn), lambda i, j: (i, j)),
        out_shape=jax.ShapeDtypeStruct((m, f), BF16),
        compiler_params=_params(("parallel", "parallel")),
        name="ffn_in",
    )(x, w, w)


def _mm_res_ln_kernel(a_ref, w_ref, x_ref, g_ref, b_ref, of_ref, ob_ref, *, nk, alpha):
    k = pl.program_id(1)
    part = jnp.dot(a_ref[...], w_ref[...], preferred_element_type=F32)

    @pl.when(k == 0)
    def _():
        of_ref[...] = part

    @pl.when(k > 0)
    def _():
        of_ref[...] += part

    @pl.when(k == nk - 1)
    def _():
        y = alpha * x_ref[...] + of_ref[...]
        mu = jnp.mean(y, axis=-1, keepdims=True)
        yc = y - mu
        var = jnp.mean(yc * yc, axis=-1, keepdims=True)
        out = yc * lax.rsqrt(var + LN_EPS) * g_ref[...] + b_ref[...]
        of_ref[...] = out
        ob_ref[...] = out.astype(BF16)


def _mm_res_ln(a, w, x, g, b, alpha, name):
    m, kdim = a.shape
    n = w.shape[1]
    tm = _pick(m, (512, 256, 128))
    tk = kdim if kdim * n * 2 <= 8 * 2 ** 20 else _pick(kdim, (1408, 1024, 512, 256, 128))
    nk = kdim // tk
    return pl.pallas_call(
        functools.partial(_mm_res_ln_kernel, nk=nk, alpha=alpha),
        grid=(m // tm, nk),
        in_specs=[pl.BlockSpec((tm, tk), lambda i, k: (i, k)),
                  pl.BlockSpec((tk, n), lambda i, k: (k, 0)),
                  pl.BlockSpec((tm, n), lambda i, k: (i, 0)),
                  pl.BlockSpec((1, n), lambda i, k: (0, 0)),
                  pl.BlockSpec((1, n), lambda i, k: (0, 0))],
        out_specs=[pl.BlockSpec((tm, n), lambda i, k: (i, 0)),
                   pl.BlockSpec((tm, n), lambda i, k: (i, 0))],
        out_shape=[jax.ShapeDtypeStruct((m, n), F32), jax.ShapeDtypeStruct((m, n), BF16)],
        compiler_params=_params(("parallel", "arbitrary")),
        name=name,
    )(a, w, x, g.reshape(1, n), b.reshape(1, n))


def _merge_kernel(oa_ref, ob_ref, wa_ref, wb_ref, ga_ref, gb_ref, o_ref):
    a = jnp.dot(oa_ref[...], wa_ref[...], preferred_element_type=F32)
    b = jnp.dot(ob_ref[...], wb_ref[...], preferred_element_type=F32)
    ga = jax.nn.sigmoid(ga_ref[...].astype(F32))
    gb = jax.nn.sigmoid(gb_ref[...].astype(F32))
    o_ref[...] = (ga * a + gb * b).astype(o_ref.dtype)


def _merge(o_a, o_b, w_a, w_b, p2, ga_col, gb_col, d_model):
    m, ka = o_a.shape
    kb = o_b.shape[1]
    tm = _pick(m, (1024, 512, 256, 128))
    tn = 512
    assert d_model % tn == 0 and ga_col % tn == 0 and gb_col % tn == 0
    ga_blk, gb_blk = ga_col // tn, gb_col // tn
    return pl.pallas_call(
        _merge_kernel,
        grid=(m // tm, d_model // tn),
        in_specs=[pl.BlockSpec((tm, ka), lambda i, j: (i, 0)),
                  pl.BlockSpec((tm, kb), lambda i, j: (i, 0)),
                  pl.BlockSpec((ka, tn), lambda i, j: (0, j)),
                  pl.BlockSpec((kb, tn), lambda i, j: (0, j)),
                  pl.BlockSpec((tm, tn), lambda i, j: (i, ga_blk + j)),
                  pl.BlockSpec((tm, tn), lambda i, j: (i, gb_blk + j))],
        out_specs=pl.BlockSpec((tm, tn), lambda i, j: (i, j)),
        out_shape=jax.ShapeDtypeStruct((m, d_model), BF16),
        compiler_params=_params(("parallel", "parallel")),
        name="merge",
    )(o_a, o_b, w_a, w_b, p2, p2)


def _softplus(x):
    return jnp.maximum(x, 0.0) + jnp.log1p(jnp.exp(-jnp.abs(x)))


def _gdn_kernel(q_ref, k_ref, v_ref, z_ref, sm_ref, cwq_ref, cwk_ref, cwv_ref, alog_ref, dtb_ref, nw_ref,
                o_ref, ext_ref, halo_ref, state_ref, u_ref, w_ref, qd_ref, kdt_ref, in_ref, gt_ref, *, nc, hp):
    h0 = pl.program_id(1) * hp
    tb = pl.program_id(2)
    t = nc * CHUNK
    dk = GDN_HEAD_DIM

    @pl.when(tb == 0)
    def _():
        halo_ref[...] = jnp.zeros_like(halo_ref)
        state_ref[...] = jnp.zeros_like(state_ref)

    def conv_silu(x_ref, cw_ref, slot):
        x = x_ref[...].astype(F32)
        ext_ref[0:8, :] = halo_ref[slot]
        ext_ref[8:8 + t, :] = x
        halo_ref[slot] = x[t - 8:t, :]
        cw = cw_ref[...]
        y = cw[CONV_WIDTH - 1:CONV_WIDTH, :] * x
        for j in range(CONV_WIDTH - 1):
            y = y + cw[j:j + 1, :] * ext_ref[pl.ds(8 - (CONV_WIDTH - 1) + j, t), :]
        return y * jax.nn.sigmoid(y)

    q = conv_silu(q_ref, cwq_ref, 0)
    k = conv_silu(k_ref, cwk_ref, 1)
    v = conv_silu(v_ref, cwv_ref, 2)

    sm = sm_ref[...]
    lane = lax.broadcasted_iota(jnp.int32, sm.shape, 1)
    g_all = -jnp.exp(alog_ref[...]) * _softplus(sm + dtb_ref[...])
    beta_all = jax.nn.sigmoid(sm)

    c3 = lambda a: a.reshape(nc, CHUNK, a.shape[-1])
    ri = lax.broadcasted_iota(jnp.int32, (CHUNK, CHUNK), 0)
    ci = lax.broadcasted_iota(jnp.int32, (CHUNK, CHUNK), 1)
    causal = (ci <= ri)[None]
    strict = (ci < ri)[None]
    eye = (ci == ri).astype(F32)[None]
    upper = (ri <= ci).astype(F32)[None]
    bmm = functools.partial(jnp.einsum, preferred_element_type=F32)

    for e in range(hp):
        hs = slice(e * dk, (e + 1) * dk)
        g_col = jnp.sum(jnp.where(lane == h0 + e, g_all, 0.0), axis=1, keepdims=True)
        beta = jnp.sum(jnp.where(lane == GDN_HEADS + h0 + e, beta_all, 0.0), axis=1, keepdims=True)
        qe, ke, ve = q[:, hs], k[:, hs], v[:, hs]
        qn = qe * lax.rsqrt(jnp.sum(qe * qe, axis=1, keepdims=True) + RMS_EPS) * (dk ** -0.5)
        kn = ke * lax.rsqrt(jnp.sum(ke * ke, axis=1, keepdims=True) + RMS_EPS)
        kb = kn * beta
        vb = ve * beta

        g3 = c3(g_col)
        gc_row = jnp.sum(g3 * upper, axis=1, keepdims=True)
        gc_col = jnp.sum(eye * gc_row, axis=2, keepdims=True)
        gc_last = gc_row[:, :, CHUNK - 1:CHUNK]
        decay = jnp.exp(jnp.where(causal, gc_col - gc_row, NEG_BIG))
        e_col = jnp.exp(gc_col)

        qn3, kn3, kb3, vb3 = c3(qn), c3(kn), c3(kb), c3(vb)
        kn3b = kn3.astype(BF16)
        kk = bmm("cid,cjd->cij", kb3.astype(BF16), kn3b)
        mneg = jnp.where(strict, -(kk * decay), 0.0)
        inv = eye + mneg
        pw = mneg.astype(BF16)
        for _ in range(int(math.log2(CHUNK)) - 1):
            pw = bmm("cij,cjk->cik", pw, pw).astype(BF16)
            inv = inv + bmm("cij,cjk->cik", inv.astype(BF16), pw)
        invb = inv.astype(BF16)
        u_ref[e] = bmm("cij,cjd->cid", invb, vb3.astype(BF16))
        w_ref[e] = bmm("cij,cjd->cid", invb, (kb3 * e_col).astype(BF16))
        in_ref[e] = bmm("cid,cjd->cij", qn3.astype(BF16), kn3b) * decay
        qd_ref[e] = qn3 * e_col
        kdt_ref[e] = jnp.swapaxes(kn3 * jnp.exp(gc_last - gc_col), 1, 2)
        gt_ref[e] = jnp.exp(gc_last)

    nw = nw_ref[...]
    for c in range(nc):
        rows = slice(c * CHUNK, (c + 1) * CHUNK)
        for e in range(hp):
            hs = slice(e * dk, (e + 1) * dk)
            s = state_ref[e]
            sb = s.astype(BF16)
            v_new = u_ref[e, c] - jnp.dot(w_ref[e, c].astype(BF16), sb, preferred_element_type=F32)
            vnb = v_new.astype(BF16)
            o = (jnp.dot(qd_ref[e, c].astype(BF16), sb, preferred_element_type=F32)
                 + jnp.dot(in_ref[e, c].astype(BF16), vnb, preferred_element_type=F32))
            state_ref[e] = s * gt_ref[e, c] + jnp.dot(kdt_ref[e, c].astype(BF16), vnb,
                                                      preferred_element_type=F32)
            zc = z_ref[rows, hs].astype(F32)
            o = o * lax.rsqrt(jnp.mean(o * o, axis=1, keepdims=True) + RMS_EPS) * nw * (zc * jax.nn.sigmoid(zc))
            o_ref[rows, hs] = o.astype(o_ref.dtype)


def _gdn(p1, ps, conv_w, a_log, dt_bias, norm_w, batch, seq):
    m = p1.shape[0]
    hd = GDN_HEAD_DIM
    nh = GDN_HEADS
    t = _pick(seq, (512, 256, 128, 64))
    nc = t // CHUNK
    nt = seq // t
    hp = GDN_HEADS_PER_STEP
    ng = nh // hp
    wd = hp * hd
    pad = lambda a: jnp.pad(a.astype(F32), (0, LANES - a.shape[0])).reshape(1, LANES)
    row = lambda b, h, i: b * nt + i
    col_spec = lambda off: pl.BlockSpec((t, wd), lambda b, h, i: (row(b, h, i), off + h))
    cw_spec = lambda off: pl.BlockSpec((CONV_WIDTH, wd), lambda b, h, i: (0, off + h))
    vec_spec = pl.BlockSpec((1, LANES), lambda b, h, i: (0, 0))
    return pl.pallas_call(
        functools.partial(_gdn_kernel, nc=nc, hp=hp),
        grid=(batch, ng, nt),
        in_specs=[col_spec(0), col_spec(ng), col_spec(2 * ng), col_spec(3 * ng),
                  pl.BlockSpec((t, LANES), lambda b, h, i: (row(b, h, i), 0)),
                  cw_spec(0), cw_spec(ng), cw_spec(2 * ng),
                  vec_spec, vec_spec, vec_spec],
        out_specs=pl.BlockSpec((t, wd), lambda b, h, i: (row(b, h, i), h)),
        out_shape=jax.ShapeDtypeStruct((m, nh * hd), BF16),
        scratch_shapes=[pltpu.VMEM((t + 8, wd), F32),
                        pltpu.VMEM((3, 8, wd), F32),
                        pltpu.VMEM((hp, hd, hd), F32),
                        pltpu.VMEM((hp, nc, CHUNK, hd), F32),
                        pltpu.VMEM((hp, nc, CHUNK, hd), F32),
                        pltpu.VMEM((hp, nc, CHUNK, hd), F32),
                        pltpu.VMEM((hp, nc, hd, CHUNK), F32),
                        pltpu.VMEM((hp, nc, CHUNK, CHUNK), F32),
                        pltpu.VMEM((hp, nc, 1, 1), F32)],
        compiler_params=_params(("parallel", "parallel", "arbitrary")),
        name="gdn",
    )(p1, p1, p1, p1, ps, conv_w, conv_w, conv_w, pad(a_log), pad(dt_bias), norm_w.reshape(1, hd).astype(F32))


def _key_to_float(key):
    bits = jnp.where(key >= 0, key, key ^ jnp.int32(0x7FFFFFFF))
    return lax.bitcast_convert_type(bits, F32)


def _upper_half(x):
    bits = lax.bitcast_convert_type(x, jnp.int32) & jnp.int32(-65536)
    return lax.bitcast_convert_type(bits, F32).astype(BF16)


def _indexer_kernel(qi_ref, ws_ref, ki_ref, mask_ref, sc_ref, sch_ref, *, tq, tk, seq, topk, w_lane):
    i = pl.program_id(1)
    t0 = i * tq
    nt = (t0 + tq + tk - 1) // tk
    n_tiles = seq // tk
    sl = 8
    ways = 4
    key0 = lax.broadcasted_iota(jnp.int32, (tk, tq), 0)
    qpos = t0 + lax.broadcasted_iota(jnp.int32, (tk, tq), 1)

    qi = qi_ref[...]
    ws_t = jnp.transpose(ws_ref[...])
    q_heads = [qi[:, h * IDX_HEAD_DIM:(h + 1) * IDX_HEAD_DIM] for h in range(IDX_HEADS)]
    w_rows = [ws_t[w_lane + h:w_lane + h + 1, :] for h in range(IDX_HEADS)]

    def score_tile(j, carry):
        s0 = pl.multiple_of(j * tk, tk)
        kt = ki_ref[pl.ds(s0, tk), :]
        acc = jnp.zeros((tk, tq), F32)
        for h in range(IDX_HEADS):
            r = lax.dot_general(kt, q_heads[h], (((1,), (1,)), ((), ())), preferred_element_type=F32)
            acc = acc + w_rows[h] * jnp.maximum(r, 0.0)
        sc = jnp.where(key0 + s0 <= qpos, acc, -jnp.inf)
        sc_ref[pl.ds(s0, tk), :] = sc
        sch_ref[pl.ds(s0, tk), :] = _upper_half(sc)
        return carry

    lax.fori_loop(0, nt, score_tile, 0)

    def count_upper(cf):
        ch = jnp.broadcast_to(_upper_half(cf), (tk, tq))
        pk = 2 * sl

        def body(j, acc):
            s0 = pl.multiple_of(j * tk, tk)
            ind = jnp.where(sch_ref[pl.ds(s0, tk), :] >= ch, jnp.bfloat16(1), jnp.bfloat16(0))
            parts = ind.reshape(tk // (pk * ways), ways, pk, tq)
            for r in range(parts.shape[0]):
                acc = acc + parts[r]
            return acc
        assert (seq // (pk * ways)) < 256
        acc = lax.fori_loop(0, nt, body, jnp.zeros((ways, pk, tq), BF16))
        return jnp.sum(jnp.sum(acc.astype(F32), axis=0), axis=0, keepdims=True)

    def count(pred):
        def body(j, acc):
            s0 = pl.multiple_of(j * tk, tk)
            ind = jnp.where(pred(sc_ref[pl.ds(s0, tk), :]), 1.0, 0.0)
            return acc + jnp.sum(ind.reshape(tk // (sl * ways), ways, sl, tq), axis=0)
        acc = lax.fori_loop(0, nt, body, jnp.zeros((ways, sl, tq), F32))
        return jnp.sum(jnp.sum(acc, axis=0), axis=0, keepdims=True)

    kf = jnp.float32(topk)
    rows = lambda a: jnp.broadcast_to(a, (tk, tq))
    c_nonneg = count_upper(jnp.zeros((1, tq), F32))
    nonneg = c_nonneg >= kf
    base0 = jnp.where(nonneg, jnp.int32(0), jnp.int32(INT_MIN))
    cnt0 = jnp.where(nonneg, c_nonneg, jnp.float32(seq + 1))

    def bit_step(count_ge):
        def step(b, carry):
            base, cnt = carry
            cand = base | jnp.left_shift(jnp.int32(1), 30 - b)
            c = count_ge(_key_to_float(cand))
            ok = c >= kf
            return jnp.where(ok, cand, base), jnp.where(ok, c, cnt)
        return step

    carry = lax.fori_loop(0, 15, bit_step(count_upper), (base0, cnt0))
    base, n_ge = lax.fori_loop(15, 31, bit_step(lambda cf: count(lambda x: x >= rows(cf))), carry)
    no_thr = base == jnp.int32(INT_MIN)
    take_all = rows(no_thr)
    thr = rows(_key_to_float(base))
    tie_excess = jnp.logical_and(jnp.logical_not(no_thr), n_ge > kf)
    any_excess = jnp.max(jnp.where(tie_excess, 1.0, 0.0)) > 0.0

    def write(j, sel):
        s0 = pl.multiple_of(j * tk, tk)
        keep = jnp.logical_and(key0 + s0 <= qpos, jnp.logical_or(take_all, sel))
        keep_qk = jnp.transpose(jnp.where(keep, 1.0, 0.0))
        mask_ref[:, pl.ds(s0, tk)] = jnp.where(keep_qk > 0.5, 1, 0).astype(jnp.int8)

    @pl.when(jnp.logical_not(any_excess))
    def _():
        def body(j, carry):
            s0 = pl.multiple_of(j * tk, tk)
            write(j, sc_ref[pl.ds(s0, tk), :] >= thr)
            return carry
        lax.fori_loop(0, nt, body, 0)

    @pl.when(any_excess)
    def _():
        need = rows(kf - count(lambda x: x > thr))
        earlier = (lax.broadcasted_iota(jnp.int32, (tk, tk), 1)
                   < lax.broadcasted_iota(jnp.int32, (tk, tk), 0)).astype(BF16)

        def body(j, seen):
            s0 = pl.multiple_of(j * tk, tk)
            x = sc_ref[pl.ds(s0, tk), :]
            eq = x == thr
            eqf = jnp.where(eq, 1.0, 0.0)
            rank = rows(seen) + jnp.dot(earlier, eqf.astype(BF16), preferred_element_type=F32)
            write(j, jnp.logical_or(x > thr, jnp.logical_and(eq, rank < need)))
            return seen + jnp.sum(eqf, axis=0, keepdims=True)
        lax.fori_loop(0, nt, body, jnp.zeros((1, tq), F32))

    def clear(j, carry):
        s0 = pl.multiple_of(j * tk, tk)
        mask_ref[:, pl.ds(s0, tk)] = jnp.zeros((tq, tk), jnp.int8)
        return carry

    lax.fori_loop(nt, n_tiles, clear, 0)


def _indexer(p2, ps, ki, qi_col, w_lane, batch, seq):
    tq = _pick(seq, (256, 128))
    tk = _pick(seq, (512, 256, 128))
    nq = seq // tq
    qw = IDX_HEADS * IDX_HEAD_DIM
    assert qi_col % qw == 0
    topk = min(TOPK_MAX, seq // 4)
    return pl.pallas_call(
        functools.partial(_indexer_kernel, tq=tq, tk=tk, seq=seq, topk=topk, w_lane=w_lane),
        grid=(batch, nq),
        in_specs=[pl.BlockSpec((tq, qw), lambda b, i: (b * nq + i, qi_col // qw)),
                  pl.BlockSpec((tq, LANES), lambda b, i: (b * nq + i, 0)),
                  pl.BlockSpec((None, seq, IDX_HEAD_DIM), lambda b, i: (b, 0, 0))],
        out_specs=pl.BlockSpec((None, tq, seq), lambda b, i: (b, i, 0)),
        out_shape=jax.ShapeDtypeStruct((batch, seq, seq), jnp.int8),
        scratch_shapes=[pltpu.VMEM((seq, tq), F32),
                        pltpu.VMEM((seq, tq), BF16)],
        compiler_params=_params(("parallel", "parallel")),
        name="indexer",
    )(p2, ps, ki)


def _bias_kernel(rb_ref, o_ref, *, t):
    d = pl.program_id(0)
    h = pl.program_id(1)
    r = lax.broadcasted_iota(jnp.int32, (t, t), 0)
    c = lax.broadcasted_iota(jnp.int32, (t, t), 1)
    n = jnp.maximum(r - c + d * t, 0)
    max_exact = REL_BUCKETS // 2
    log_ratio = jnp.log(jnp.maximum(n, 1).astype(F32) / max_exact) / math.log(REL_MAX_DIST / max_exact)
    large = max_exact + (log_ratio * (REL_BUCKETS - max_exact)).astype(jnp.int32)
    large = jnp.minimum(large, REL_BUCKETS - 1)
    bucket = jnp.where(n < max_exact, n, large)
    out = jnp.zeros((t, t), F32)
    for b in range(REL_BUCKETS):
        out = jnp.where(bucket == b, rb_ref[b, h], out)
    o_ref[...] = out * LOG2E


def _bias_tiles(rel_bias, t):
    assert t >= REL_MAX_DIST
    return pl.pallas_call(
        functools.partial(_bias_kernel, t=t),
        grid=(3, DSA_HEADS),
        in_specs=[pl.BlockSpec(memory_space=pltpu.SMEM)],
        out_specs=pl.BlockSpec((None, None, t, t), lambda d, h: (d, h, 0, 0)),
        out_shape=jax.ShapeDtypeStruct((3, DSA_HEADS, t, t), F32),
        compiler_params=_params(("parallel", "parallel")),
        name="rel_bias_tiles",
    )(rel_bias.astype(F32))


def _attn_kernel(qi_ref, kj_ref, q_ref, k_ref, v_ref, mask_ref, bias_ref, o_ref,
                 m_ref, acc_ref, vaug_ref, madd_ref, *, tq, tk):
    pair = pl.program_id(1)
    i = qi_ref[pair]
    j = kj_ref[pair]
    hd = DSA_HEAD_DIM
    bt = BIAS_T
    first = j == 0
    last = j == (i * tq + tq - 1) // tk

    @pl.when(pair == 0)
    def _():
        vaug_ref[...] = jnp.ones_like(vaug_ref)

    @pl.when(first)
    def _():
        m_ref[...] = jnp.full_like(m_ref, NEG_BIG)
        acc_ref[...] = jnp.zeros_like(acc_ref)

    def stage():
        madd_ref[...] = jnp.where(mask_ref[...].astype(jnp.int32) != 0, 0.0, NEG_BIG)
        for h in range(DSA_HEADS):
            vaug_ref[:, 2 * h * hd:(2 * h + 1) * hd] = v_ref[:, h * hd:(h + 1) * hd]

    lead = (i * tq - j * tk) // bt
    kinds = [[jnp.clip(lead + a - c, 0, 2) for c in range(tk // bt)] for a in range(tq // bt)]
    far = lead - (tk // bt - 1) >= 2

    def head_step(h, far_tile):
        sl = slice(h * hd, (h + 1) * hd)
        s = lax.dot_general(q_ref[:, sl], k_ref[:, sl], (((1,), (1,)), ((), ())),
                            preferred_element_type=F32)
        cols = []
        for c in range(tk // LANES):
            cs = slice(c * LANES, (c + 1) * LANES)
            blk, off = divmod(c * LANES, bt)
            t = s[:, cs] + madd_ref[:, cs]
            if not far_tile:
                t = t + jnp.concatenate([bias_ref[kinds[a][blk], h, :, off:off + LANES]
                                         for a in range(tq // bt)], axis=0)
            cols.append(t)
        mx = jnp.max(functools.reduce(jnp.maximum, cols), axis=1, keepdims=True)
        m_prev = m_ref[h]
        if far_tile:
            cb = jnp.concatenate([bias_ref[2, h, :, 0:LANES]] * (tq // bt), axis=0)
            m_new = jnp.maximum(m_prev, mx + cb)
            shift = m_new - cb
        else:
            m_new = jnp.maximum(m_prev, mx)
            shift = m_new
        alpha = jnp.exp2(m_prev - m_new)
        p = jnp.concatenate([jnp.exp2(t - shift) for t in cols], axis=1).astype(BF16)
        pv = jnp.dot(p, vaug_ref[:, 2 * h * hd:(2 * h + 2) * hd], preferred_element_type=F32)
        for half in range(2):
            a = slice((2 * h + half) * hd, (2 * h + half + 1) * hd)
            acc_ref[:, a] = alpha * acc_ref[:, a] + pv[:, half * hd:(half + 1) * hd]
        m_ref[h] = m_new

    @pl.when(far)
    def _():
        stage()
        for h in range(DSA_HEADS):
            head_step(h, True)

    @pl.when(jnp.logical_not(far))
    def _():
        stage()
        for h in range(DSA_HEADS):
            head_step(h, False)

    @pl.when(last)
    def _():
        for h in range(DSA_HEADS):
            num = acc_ref[:, 2 * h * hd:(2 * h + 1) * hd]
            den = acc_ref[:, (2 * h + 1) * hd:(2 * h + 2) * hd]
            o_ref[:, h * hd:(h + 1) * hd] = (num / den).astype(o_ref.dtype)


def _attention(p2, mask, bias, q_col, k_col, v_col, batch, seq):
    m = p2.shape[0]
    width = DSA_HEADS * DSA_HEAD_DIM
    tq = _pick(seq, (512, 256))
    tk = _pick(seq, (512, 256))
    assert tq % BIAS_T == 0 and tk % BIAS_T == 0 and DSA_HEAD_DIM == LANES
    nq, nk = seq // tq, seq // tk
    assert q_col % width == 0 and k_col % width == 0 and v_col % width == 0
    qb, kb, vb = q_col // width, k_col // width, v_col // width
    pairs = [(i, j) for i in range(nq) for j in range((i * tq + tq - 1) // tk + 1)]
    qi = jnp.asarray([p[0] for p in pairs], jnp.int32)
    kj = jnp.asarray([p[1] for p in pairs], jnp.int32)
    grid_spec = pltpu.PrefetchScalarGridSpec(
        num_scalar_prefetch=2,
        grid=(batch, len(pairs)),
        in_specs=[pl.BlockSpec((tq, width), lambda b, p, qi, kj: (b * nq + qi[p], qb)),
                  pl.BlockSpec((tk, width), lambda b, p, qi, kj: (b * nk + kj[p], kb)),
                  pl.BlockSpec((tk, width), lambda b, p, qi, kj: (b * nk + kj[p], vb)),
                  pl.BlockSpec((None, tq, tk), lambda b, p, qi, kj: (b, qi[p], kj[p])),
                  pl.BlockSpec((3, DSA_HEADS, BIAS_T, BIAS_T), lambda b, p, qi, kj: (0, 0, 0, 0))],
        out_specs=pl.BlockSpec((tq, width), lambda b, p, qi, kj: (b * nq + qi[p], 0)),
        scratch_shapes=[pltpu.VMEM((DSA_HEADS, tq, LANES), F32),
                        pltpu.VMEM((tq, 2 * width), F32),
                        pltpu.VMEM((tk, 2 * width), BF16),
                        pltpu.VMEM((tq, tk), F32)])
    return pl.pallas_call(
        functools.partial(_attn_kernel, tq=tq, tk=tk),
        grid_spec=grid_spec,
        out_shape=jax.ShapeDtypeStruct((m, width), BF16),
        compiler_params=_params(("parallel", "arbitrary")),
        name="sparse_attn",
    )(qi, kj, p2, p2, p2, mask, bias)


def _cast_kernel(x_ref, o_ref):
    o_ref[...] = x_ref[...].astype(o_ref.dtype)


def _to_bf16(w, layer, name):
    _, r, c = w.shape
    rb = next(b for b in (1024, 512, 256, 128, 64, 32, 16) if r % b == 0 and b * c * 4 <= 8 * 2 ** 20)
    return pl.pallas_call(
        _cast_kernel,
        grid=(r // rb,),
        in_specs=[pl.BlockSpec((None, rb, c), lambda i: (layer, i, 0))],
        out_specs=pl.BlockSpec((rb, c), lambda i: (i, 0)),
        out_shape=jax.ShapeDtypeStruct((r, c), BF16),
        compiler_params=_params(("parallel",)),
        name=name,
    )(w)


def _regroup_kernel(w_ref, og_ref, os_ref, or_ref, *, offs, q_scale):
    x = w_ref[...]
    part = lambda i: x[:, offs[i]:offs[i + 1]]
    qkv_a, a_in, b_in, z, q_b, k_b, v_b, q_i, k_i, w_i, gate_a, gate_b = (part(i) for i in range(12))
    fill = jnp.zeros((x.shape[0], LANES - 2 * GDN_HEADS - IDX_HEADS - IDX_HEAD_DIM), x.dtype)
    og_ref[...] = jnp.concatenate([qkv_a, z], axis=1).astype(BF16)
    os_ref[...] = jnp.concatenate([a_in, b_in, w_i, fill, k_i], axis=1).astype(BF16)
    or_ref[...] = jnp.concatenate([q_b * q_scale, k_b, v_b, q_i, gate_a, gate_b], axis=1).astype(BF16)


def _split_w_in(w, layer, d_model):
    gw = GDN_HEADS * GDN_HEAD_DIM
    dw = DSA_HEADS * DSA_HEAD_DIM
    iw = IDX_HEADS * IDX_HEAD_DIM
    sizes = (3 * gw, GDN_HEADS, GDN_HEADS, gw, dw, dw, dw, iw, IDX_HEAD_DIM, IDX_HEADS, d_model, d_model)
    offs = [0]
    for s in sizes:
        offs.append(offs[-1] + s)
    _, r, d_in = w.shape
    assert offs[-1] == d_in
    rb = _pick(r, (128, 64, 32, 16))
    n_gdn, n_rest = 4 * gw, 3 * dw + iw + 2 * d_model
    w_gdn, w_small, w_rest = pl.pallas_call(
        functools.partial(_regroup_kernel, offs=tuple(offs), q_scale=DSA_HEAD_DIM ** -0.5 * LOG2E),
        grid=(r // rb,),
        in_specs=[pl.BlockSpec((None, rb, d_in), lambda i: (layer, i, 0))],
        out_specs=[pl.BlockSpec((rb, n_gdn), lambda i: (i, 0)),
                   pl.BlockSpec((rb, LANES), lambda i: (i, 0)),
                   pl.BlockSpec((rb, n_rest), lambda i: (i, 0))],
        out_shape=[jax.ShapeDtypeStruct((r, n_gdn), BF16),
                   jax.ShapeDtypeStruct((r, LANES), BF16),
                   jax.ShapeDtypeStruct((r, n_rest), BF16)],
        compiler_params=_params(("parallel",)),
        name="regroup_w_in",
    )(w)
    cols = dict(q=0, k=dw, v=2 * dw, qi=3 * dw, ga=3 * dw + iw, gb=3 * dw + iw + d_model,
                w_lane=2 * GDN_HEADS, ki_lane=LANES - IDX_HEAD_DIM)
    return w_gdn, w_small, w_rest, cols


@jax.jit
def _forward(x, rel_bias, w_in, conv_w, a_log, dt_bias, gdn_norm_w, w_branch_a, w_branch_b, w_out,
             ln1_g, ln1_b, w_ffn_in, w_ffn_out, ln2_g, ln2_b):
    batch, seq, d_model = x.shape
    depth = w_in.shape[0]
    m = batch * seq
    alpha = (2 * depth) ** 0.25
    bias = _bias_tiles(rel_bias, BIAS_T)
    xf = x.reshape(m, d_model).astype(F32)
    xb = _to_bf16(xf.reshape(1, m, d_model), 0, "cast_x")
    for l in range(depth):
        w_gdn, w_small, w_rest, cols = _split_w_in(w_in, l, d_model)
        p1 = _matmul(xb, w_gdn, BF16, "proj_gdn")
        ps = _matmul(xb, w_small, F32, "proj_small")
        p2 = _matmul(xb, w_rest, BF16, "proj_rest")
        o_a = _gdn(p1, ps, conv_w[l], a_log[l], dt_bias[l], gdn_norm_w[l], batch, seq)
        ki = ps[:, cols["ki_lane"]:].astype(BF16).reshape(batch, seq, IDX_HEAD_DIM)
        mask = _indexer(p2, ps, ki, cols["qi"], cols["w_lane"], batch, seq)
        o_b = _attention(p2, mask, bias, cols["q"], cols["k"], cols["v"], batch, seq)
        merged = _merge(o_a, o_b, _to_bf16(w_branch_a, l, "cast_w_a"), _to_bf16(w_branch_b, l, "cast_w_b"), p2,
                        cols["ga"], cols["gb"], d_model)
        xf, xb = _mm_res_ln(merged, _to_bf16(w_out, l, "cast_w_out"), xf, ln1_g[l], ln1_b[l], alpha,
                            "out_proj_ln")
        act = _ffn_in(xb, _to_bf16(w_ffn_in, l, "cast_w_ffn_in"))
        xf, xb = _mm_res_ln(act, _to_bf16(w_ffn_out, l, "cast_w_ffn_out"), xf, ln2_g[l], ln2_b[l], alpha,
                            "ffn_out_ln")
    return xf.reshape(batch, seq, d_model).astype(x.dtype)


def kernel(x, rel_bias, w_in, conv_w, a_log, dt_bias, gdn_norm_w, w_branch_a, w_branch_b, w_out, ln1_g, ln1_b,
           w_ffn_in, w_ffn_out, ln2_g, ln2_b):
    return _forward(x, rel_bias, w_in, conv_w, a_log, dt_bias, gdn_norm_w, w_branch_a, w_branch_b, w_out,
                    ln1_g, ln1_b, w_ffn_in, w_ffn_out, ln2_g, ln2_b)
```

```python
import functools
import math

import jax
import jax.numpy as jnp
from jax import lax
from jax.experimental import pallas as pl
from jax.experimental.pallas import tpu as pltpu

F32 = jnp.float32
BF16 = jnp.bfloat16

GDN_HEADS = 8
GDN_HEAD_DIM = 128
CONV_WIDTH = 4
CHUNK = 64
DSA_HEADS = 8
DSA_HEAD_DIM = 128
IDX_HEADS = 8
IDX_HEAD_DIM = 64
TOPK_MAX = 256
REL_BUCKETS = 32
REL_MAX_DIST = 128
LN_EPS = 1e-5
RMS_EPS = 1e-6

LANES = 128
VMEM_LIMIT = 56 * 1024 * 1024

NEG_BIG = -1e30
INT_MIN = -(2 ** 31)
LOG2E = math.log2(math.e)
BIAS_T = 256
EARLY_EXIT_PASS = 23
GDN_HEADS_PER_STEP = 4


def _params(sem):
    return pltpu.CompilerParams(dimension_semantics=sem, vmem_limit_bytes=VMEM_LIMIT)


def _pick(n, prefs):
    for p in prefs:
        if n % p == 0:
            return p
    return n


def _mm_kernel(x_ref, w_ref, o_ref):
    o_ref[...] = jnp.dot(x_ref[...], w_ref[...], preferred_element_type=F32).astype(o_ref.dtype)


def _matmul(x, w, out_dtype, name):
    m, k = x.shape
    n = w.shape[1]
    tm = _pick(m, (1024, 512, 256, 128))
    tn = _pick(n, (512, 256, 128))
    return pl.pallas_call(
        _mm_kernel,
        grid=(m // tm, n // tn),
        in_specs=[pl.BlockSpec((tm, k), lambda i, j: (i, 0)),
                  pl.BlockSpec((k, tn), lambda i, j: (0, j))],
        out_specs=pl.BlockSpec((tm, tn), lambda i, j: (i, j)),
        out_shape=jax.ShapeDtypeStruct((m, n), out_dtype),
        compiler_params=_params(("parallel", "parallel")),
        name=name,
    )(x, w)


def _ffn_in_kernel(x_ref, wg_ref, wu_ref, o_ref):
    x = x_ref[...]
    g = jnp.dot(x, wg_ref[...], preferred_element_type=F32)
    u = jnp.dot(x, wu_ref[...], preferred_element_type=F32)
    o_ref[...] = (g * jax.nn.sigmoid(g) * u).astype(o_ref.dtype)


def _ffn_in(x, w):
    m, k = x.shape
    f = w.shape[1] // 2
    tm = _pick(m, (1024, 512, 256, 128))
    tn = _pick(f, (512, 256, 128))
    nb = f // tn
    return pl.pallas_call(
        _ffn_in_kernel,
        grid=(m // tm, nb),
        in_specs=[pl.BlockSpec((tm, k), lambda i, j: (i, 0)),
                  pl.BlockSpec((k, tn), lambda i, j: (0, j)),
                  pl.BlockSpec((k, tn), lambda i, j: (0, j + nb))],
        out_specs=pl.BlockSpec((tm, tn), lambda i, j: (i, j)),
        out_shape=jax.ShapeDtypeStruct((m, f), BF16),
        compiler_params=_params(("parallel", "parallel")),
        name="ffn_in",
    )(x, w, w)


def _mm_res_ln_kernel(a_ref, w_ref, x_ref, g_ref, b_ref, of_ref, ob_ref, *acc, nk, alpha):
    def finish(prod):
        y = alpha * x_ref[...] + prod
        mu = jnp.mean(y, axis=-1, keepdims=True)
        yc = y - mu
        var = jnp.mean(yc * yc, axis=-1, keepdims=True)
        out = yc * lax.rsqrt(var + LN_EPS) * g_ref[...] + b_ref[...]
        of_ref[...] = out
        ob_ref[...] = out.astype(BF16)

    if nk == 1:
        finish(jnp.dot(a_ref[...], w_ref[...], preferred_element_type=F32))
        return
    acc_ref, = acc
    k = pl.program_id(1)

    @pl.when(k == 0)
    def _():
        acc_ref[...] = jnp.zeros_like(acc_ref)

    acc_ref[...] += jnp.dot(a_ref[...], w_ref[...], preferred_element_type=F32)

    @pl.when(k == nk - 1)
    def _():
        finish(acc_ref[...])


def _mm_res_ln(a, w, x, g, b, alpha, name):
    m, kdim = a.shape
    n = w.shape[1]
    tm = _pick(m, (512, 256, 128))
    tk = kdim if kdim * n * 2 <= 8 * 2 ** 20 else _pick(kdim, (1408, 1024, 512, 256, 128))
    nk = kdim // tk
    return pl.pallas_call(
        functools.partial(_mm_res_ln_kernel, nk=nk, alpha=alpha),
        grid=(m // tm, nk),
        in_specs=[pl.BlockSpec((tm, tk), lambda i, k: (i, k)),
                  pl.BlockSpec((tk, n), lambda i, k: (k, 0)),
                  pl.BlockSpec((tm, n), lambda i, k: (i, 0)),
                  pl.BlockSpec((1, n), lambda i, k: (0, 0)),
                  pl.BlockSpec((1, n), lambda i, k: (0, 0))],
        out_specs=[pl.BlockSpec((tm, n), lambda i, k: (i, 0)),
                   pl.BlockSpec((tm, n), lambda i, k: (i, 0))],
        out_shape=[jax.ShapeDtypeStruct((m, n), F32), jax.ShapeDtypeStruct((m, n), BF16)],
        scratch_shapes=[pltpu.VMEM((tm, n), F32)] if nk > 1 else [],
        compiler_params=_params(("parallel", "arbitrary")),
        name=name,
    )(a, w, x, g.reshape(1, n), b.reshape(1, n))


def _merge_kernel(oa_ref, ob_ref, wa_ref, wb_ref, ga_ref, gb_ref, o_ref):
    a = jnp.dot(oa_ref[...], wa_ref[...], preferred_element_type=F32)
    b = jnp.dot(ob_ref[...], wb_ref[...], preferred_element_type=F32)
    ga = jax.nn.sigmoid(ga_ref[...].astype(F32))
    gb = jax.nn.sigmoid(gb_ref[...].astype(F32))
    o_ref[...] = (ga * a + gb * b).astype(o_ref.dtype)


def _merge(o_a, o_b, w_a, w_b, p2, ga_col, gb_col, d_model):
    m, ka = o_a.shape
    kb = o_b.shape[1]
    tm = _pick(m, (1024, 512, 256, 128))
    tn = 512
    assert d_model % tn == 0 and ga_col % tn == 0 and gb_col % tn == 0
    ga_blk, gb_blk = ga_col // tn, gb_col // tn
    return pl.pallas_call(
        _merge_kernel,
        grid=(m // tm, d_model // tn),
        in_specs=[pl.BlockSpec((tm, ka), lambda i, j: (i, 0)),
                  pl.BlockSpec((tm, kb), lambda i, j: (i, 0)),
                  pl.BlockSpec((ka, tn), lambda i, j: (0, j)),
                  pl.BlockSpec((kb, tn), lambda i, j: (0, j)),
                  pl.BlockSpec((tm, tn), lambda i, j: (i, ga_blk + j)),
                  pl.BlockSpec((tm, tn), lambda i, j: (i, gb_blk + j))],
        out_specs=pl.BlockSpec((tm, tn), lambda i, j: (i, j)),
        out_shape=jax.ShapeDtypeStruct((m, d_model), BF16),
        compiler_params=_params(("parallel", "parallel")),
        name="merge",
    )(o_a, o_b, w_a, w_b, p2, p2)


def _softplus(x):
    return jnp.maximum(x, 0.0) + jnp.log1p(jnp.exp(-jnp.abs(x)))


def _gdn_kernel(q_ref, k_ref, v_ref, z_ref, sm_ref, cwq_ref, cwk_ref, cwv_ref, alog_ref, dtb_ref, nw_ref,
                o_ref, ext_ref, halo_ref, state_ref, u_ref, w_ref, qd_ref, kdt_ref, in_ref, gt_ref, *, nc, hp):
    h0 = pl.program_id(1) * hp
    tb = pl.program_id(2)
    t = nc * CHUNK
    dk = GDN_HEAD_DIM

    @pl.when(tb == 0)
    def _():
        halo_ref[...] = jnp.zeros_like(halo_ref)
        state_ref[...] = jnp.zeros_like(state_ref)

    def conv_silu(x_ref, cw_ref, slot):
        x = x_ref[...].astype(F32)
        ext_ref[0:8, :] = halo_ref[slot]
        ext_ref[8:8 + t, :] = x
        halo_ref[slot] = x[t - 8:t, :]
        cw = cw_ref[...]
        y = cw[CONV_WIDTH - 1:CONV_WIDTH, :] * x
        for j in range(CONV_WIDTH - 1):
            y = y + cw[j:j + 1, :] * ext_ref[pl.ds(8 - (CONV_WIDTH - 1) + j, t), :]
        return y * jax.nn.sigmoid(y)

    q = conv_silu(q_ref, cwq_ref, 0)
    k = conv_silu(k_ref, cwk_ref, 1)
    v = conv_silu(v_ref, cwv_ref, 2)

    sm = sm_ref[...]
    lane = lax.broadcasted_iota(jnp.int32, sm.shape, 1)
    g_all = -jnp.exp(alog_ref[...]) * _softplus(sm + dtb_ref[...])
    beta_all = jax.nn.sigmoid(sm)

    c3 = lambda a: a.reshape(nc, CHUNK, a.shape[-1])
    ri = lax.broadcasted_iota(jnp.int32, (CHUNK, CHUNK), 0)
    ci = lax.broadcasted_iota(jnp.int32, (CHUNK, CHUNK), 1)
    causal = (ci <= ri)[None]
    strict = (ci < ri)[None]
    eye = (ci == ri).astype(F32)[None]
    upper = (ri <= ci).astype(F32)[None]
    bmm = functools.partial(jnp.einsum, preferred_element_type=F32)

    for e in range(hp):
        hs = slice(e * dk, (e + 1) * dk)
        g_col = jnp.sum(jnp.where(lane == h0 + e, g_all, 0.0), axis=1, keepdims=True)
        beta = jnp.sum(jnp.where(lane == GDN_HEADS + h0 + e, beta_all, 0.0), axis=1, keepdims=True)
        qe, ke, ve = q[:, hs], k[:, hs], v[:, hs]
        qn = qe * lax.rsqrt(jnp.sum(qe * qe, axis=1, keepdims=True) + RMS_EPS) * (dk ** -0.5)
        kn = ke * lax.rsqrt(jnp.sum(ke * ke, axis=1, keepdims=True) + RMS_EPS)
        kb = kn * beta
        vb = ve * beta

        g3 = c3(g_col)
        gc_row = jnp.sum(g3 * upper, axis=1, keepdims=True)
        gc_col = jnp.sum(eye * gc_row, axis=2, keepdims=True)
        gc_last = gc_row[:, :, CHUNK - 1:CHUNK]
        decay = jnp.exp(jnp.where(causal, gc_col - gc_row, NEG_BIG))
        e_col = jnp.exp(gc_col)

        qn3, kn3, kb3, vb3 = c3(qn), c3(kn), c3(kb), c3(vb)
        kn3b = kn3.astype(BF16)
        kk = bmm("cid,cjd->cij", kb3.astype(BF16), kn3b)
        mneg = jnp.where(strict, -(kk * decay), 0.0)
        inv = eye + mneg
        pw = mneg.astype(BF16)
        for _ in range(int(math.log2(CHUNK)) - 1):
            pw = bmm("cij,cjk->cik", pw, pw).astype(BF16)
            inv = inv + bmm("cij,cjk->cik", inv.astype(BF16), pw)
        invb = inv.astype(BF16)
        u_ref[e] = bmm("cij,cjd->cid", invb, vb3.astype(BF16))
        w_ref[e] = bmm("cij,cjd->cid", invb, (kb3 * e_col).astype(BF16))
        in_ref[e] = bmm("cid,cjd->cij", qn3.astype(BF16), kn3b) * decay
        qd_ref[e] = qn3 * e_col
        kdt_ref[e] = jnp.swapaxes(kn3 * jnp.exp(gc_last - gc_col), 1, 2)
        gt_ref[e] = jnp.exp(gc_last)

    nw = nw_ref[...]
    for c in range(nc):
        rows = slice(c * CHUNK, (c + 1) * CHUNK)
        for e in range(hp):
            hs = slice(e * dk, (e + 1) * dk)
            s = state_ref[e]
            sb = s.astype(BF16)
            v_new = u_ref[e, c] - jnp.dot(w_ref[e, c].astype(BF16), sb, preferred_element_type=F32)
            vnb = v_new.astype(BF16)
            o = (jnp.dot(qd_ref[e, c].astype(BF16), sb, preferred_element_type=F32)
                 + jnp.dot(in_ref[e, c].astype(BF16), vnb, preferred_element_type=F32))
            state_ref[e] = s * gt_ref[e, c] + jnp.dot(kdt_ref[e, c].astype(BF16), vnb,
                                                      preferred_element_type=F32)
            zc = z_ref[rows, hs].astype(F32)
            o = o * lax.rsqrt(jnp.mean(o * o, axis=1, keepdims=True) + RMS_EPS) * nw * (zc * jax.nn.sigmoid(zc))
            o_ref[rows, hs] = o.astype(o_ref.dtype)


def _gdn(p1, ps, conv_w, a_log, dt_bias, norm_w, batch, seq):
    m = p1.shape[0]
    hd = GDN_HEAD_DIM
    nh = GDN_HEADS
    t = _pick(seq, (512, 256, 128, 64))
    nc = t // CHUNK
    nt = seq // t
    hp = GDN_HEADS_PER_STEP
    ng = nh // hp
    wd = hp * hd
    pad = lambda a: jnp.pad(a.astype(F32), (0, LANES - a.shape[0])).reshape(1, LANES)
    row = lambda b, h, i: b * nt + i
    col_spec = lambda off: pl.BlockSpec((t, wd), lambda b, h, i: (row(b, h, i), off + h))
    cw_spec = lambda off: pl.BlockSpec((CONV_WIDTH, wd), lambda b, h, i: (0, off + h))
    vec_spec = pl.BlockSpec((1, LANES), lambda b, h, i: (0, 0))
    return pl.pallas_call(
        functools.partial(_gdn_kernel, nc=nc, hp=hp),
        grid=(batch, ng, nt),
        in_specs=[col_spec(0), col_spec(ng), col_spec(2 * ng), col_spec(3 * ng),
                  pl.BlockSpec((t, LANES), lambda b, h, i: (row(b, h, i), 0)),
                  cw_spec(0), cw_spec(ng), cw_spec(2 * ng),
                  vec_spec, vec_spec, vec_spec],
        out_specs=pl.BlockSpec((t, wd), lambda b, h, i: (row(b, h, i), h)),
        out_shape=jax.ShapeDtypeStruct((m, nh * hd), BF16),
        scratch_shapes=[pltpu.VMEM((t + 8, wd), F32),
                        pltpu.VMEM((3, 8, wd), F32),
                        pltpu.VMEM((hp, hd, hd), F32),
                        pltpu.VMEM((hp, nc, CHUNK, hd), F32),
                        pltpu.VMEM((hp, nc, CHUNK, hd), F32),
                        pltpu.VMEM((hp, nc, CHUNK, hd), F32),
                        pltpu.VMEM((hp, nc, hd, CHUNK), F32),
                        pltpu.VMEM((hp, nc, CHUNK, CHUNK), F32),
                        pltpu.VMEM((hp, nc, 1, 1), F32)],
        compiler_params=_params(("parallel", "parallel", "arbitrary")),
        name="gdn",
    )(p1, p1, p1, p1, ps, conv_w, conv_w, conv_w, pad(a_log), pad(dt_bias), norm_w.reshape(1, hd).astype(F32))


def _key_to_float(key):
    bits = jnp.where(key >= 0, key, key ^ jnp.int32(0x7FFFFFFF))
    return lax.bitcast_convert_type(bits, F32)


def _upper_half(x):
    bits = lax.bitcast_convert_type(x, jnp.int32) & jnp.int32(-65536)
    return lax.bitcast_convert_type(bits, F32).astype(BF16)


def _indexer_kernel(qi_ref, ws_ref, ki_ref, mask_ref, sc_ref, sch_ref, *, tq, tk, seq, topk, w_lane):
    i = pl.program_id(1)
    t0 = i * tq
    nt = (t0 + tq + tk - 1) // tk
    n_tiles = seq // tk
    sl = 8
    ways = 2
    ways16 = 4
    key0 = lax.broadcasted_iota(jnp.int32, (tk, tq), 0)
    qpos = t0 + lax.broadcasted_iota(jnp.int32, (tk, tq), 1)

    qi = qi_ref[...]
    ws_t = jnp.transpose(ws_ref[...])
    q_heads = [qi[:, h * IDX_HEAD_DIM:(h + 1) * IDX_HEAD_DIM] for h in range(IDX_HEADS)]
    w_rows = [ws_t[w_lane + h:w_lane + h + 1, :] for h in range(IDX_HEADS)]

    def score_tile(j, carry):
        s0 = pl.multiple_of(j * tk, tk)
        kt = ki_ref[pl.ds(s0, tk), :]
        acc = jnp.zeros((tk, tq), F32)
        for h in range(IDX_HEADS):
            r = lax.dot_general(kt, q_heads[h], (((1,), (1,)), ((), ())), preferred_element_type=F32)
            acc = acc + w_rows[h] * jnp.maximum(r, 0.0)
        sc = jnp.where(key0 + s0 <= qpos, acc, -jnp.inf)
        sc_ref[pl.ds(s0, tk), :] = sc
        sch_ref[pl.ds(s0, tk), :] = _upper_half(sc)
        return carry

    lax.fori_loop(0, nt, score_tile, 0)

    def count_upper(cf):
        ch = jnp.broadcast_to(_upper_half(cf), (tk, tq))
        pk = 2 * sl

        def body(j, acc):
            s0 = pl.multiple_of(j * tk, tk)
            ind = jnp.where(sch_ref[pl.ds(s0, tk), :] >= ch, jnp.bfloat16(1), jnp.bfloat16(0))
            parts = ind.reshape(tk // (pk * ways16), ways16, pk, tq)
            for r in range(parts.shape[0]):
                acc = acc + parts[r]
            return acc
        assert (seq // (pk * ways16)) < 256
        acc = lax.fori_loop(0, nt, body, jnp.zeros((ways16, pk, tq), BF16))
        return jnp.sum(jnp.sum(acc.astype(F32), axis=0), axis=0, keepdims=True)

    def count(pred):
        def body(j, acc):
            s0 = pl.multiple_of(j * tk, tk)
            ind = jnp.where(pred(sc_ref[pl.ds(s0, tk), :]), 1.0, 0.0)
            return acc + jnp.sum(ind.reshape(tk // (sl * ways), ways, sl, tq), axis=0)
        acc = lax.fori_loop(0, nt, body, jnp.zeros((ways, sl, tq), F32))
        return jnp.sum(jnp.sum(acc, axis=0), axis=0, keepdims=True)

    kf = jnp.float32(topk)
    rows = lambda a: jnp.broadcast_to(a, (tk, tq))
    c_nonneg = count_upper(jnp.zeros((1, tq), F32))
    nonneg = c_nonneg >= kf
    base0 = jnp.where(nonneg, jnp.int32(0), jnp.int32(INT_MIN))
    cnt0 = jnp.where(nonneg, c_nonneg, jnp.float32(seq + 1))

    def bit_step(count_ge):
        def step(b, carry):
            base, cnt = carry
            cand = base | jnp.left_shift(jnp.int32(1), 30 - b)
            c = count_ge(_key_to_float(cand))
            ok = c >= kf
            return jnp.where(ok, cand, base), jnp.where(ok, c, cnt)
        return step

    fine_step = bit_step(lambda cf: count(lambda x: x >= rows(cf)))
    carry = lax.fori_loop(0, 15, bit_step(count_upper), (base0, cnt0))
    carry = lax.fori_loop(15, EARLY_EXIT_PASS, fine_step, carry)
    unsettled = lambda s: jnp.logical_and(s[0] < 31, jnp.max(jnp.where(s[2] != kf, 1.0, 0.0)) > 0.0)
    _, base, n_ge = lax.while_loop(unsettled, lambda s: (s[0] + 1,) + fine_step(s[0], s[1:]),
                                   (jnp.int32(EARLY_EXIT_PASS),) + carry)
    no_thr = base == jnp.int32(INT_MIN)
    take_all = rows(no_thr)
    thr = rows(_key_to_float(base))
    tie_excess = jnp.logical_and(jnp.logical_not(no_thr), n_ge > kf)
    any_excess = jnp.max(jnp.where(tie_excess, 1.0, 0.0)) > 0.0

    def write(j, sel):
        s0 = pl.multiple_of(j * tk, tk)
        keep = jnp.logical_and(key0 + s0 <= qpos, jnp.logical_or(take_all, sel))
        keep_qk = jnp.transpose(jnp.where(keep, 1.0, 0.0))
        mask_ref[:, pl.ds(s0, tk)] = jnp.where(keep_qk > 0.5, 1, 0).astype(jnp.int8)

    @pl.when(jnp.logical_not(any_excess))
    def _():
        def body(j, carry):
            s0 = pl.multiple_of(j * tk, tk)
            write(j, sc_ref[pl.ds(s0, tk), :] >= thr)
            return carry
        lax.fori_loop(0, nt, body, 0)

    @pl.when(any_excess)
    def _():
        need = rows(kf - count(lambda x: x > thr))
        earlier = (lax.broadcasted_iota(jnp.int32, (tk, tk), 1)
                   < lax.broadcasted_iota(jnp.int32, (tk, tk), 0)).astype(BF16)

        def body(j, seen):
            s0 = pl.multiple_of(j * tk, tk)
            x = sc_ref[pl.ds(s0, tk), :]
            eq = x == thr
            eqf = jnp.where(eq, 1.0, 0.0)
            rank = rows(seen) + jnp.dot(earlier, eqf.astype(BF16), preferred_element_type=F32)
            write(j, jnp.logical_or(x > thr, jnp.logical_and(eq, rank < need)))
            return seen + jnp.sum(eqf, axis=0, keepdims=True)
        lax.fori_loop(0, nt, body, jnp.zeros((1, tq), F32))

    def clear(j, carry):
        s0 = pl.multiple_of(j * tk, tk)
        mask_ref[:, pl.ds(s0, tk)] = jnp.zeros((tq, tk), jnp.int8)
        return carry

    lax.fori_loop(nt, n_tiles, clear, 0)


def _indexer(p2, ps, ki, qi_col, w_lane, batch, seq):
    tq = _pick(seq, (512, 256, 128))
    tk = _pick(seq, (512, 256, 128))
    nq = seq // tq
    qw = IDX_HEADS * IDX_HEAD_DIM
    assert qi_col % qw == 0
    topk = min(TOPK_MAX, seq // 4)
    return pl.pallas_call(
        functools.partial(_indexer_kernel, tq=tq, tk=tk, seq=seq, topk=topk, w_lane=w_lane),
        grid=(batch, nq),
        in_specs=[pl.BlockSpec((tq, qw), lambda b, i: (b * nq + i, qi_col // qw)),
                  pl.BlockSpec((tq, LANES), lambda b, i: (b * nq + i, 0)),
                  pl.BlockSpec((None, seq, IDX_HEAD_DIM), lambda b, i: (b, 0, 0))],
        out_specs=pl.BlockSpec((None, tq, seq), lambda b, i: (b, i, 0)),
        out_shape=jax.ShapeDtypeStruct((batch, seq, seq), jnp.int8),
        scratch_shapes=[pltpu.VMEM((seq, tq), F32),
                        pltpu.VMEM((seq, tq), BF16)],
        compiler_params=_params(("parallel", "parallel")),
        name="indexer",
    )(p2, ps, ki)


def _bias_kernel(rb_ref, o_ref, *, t):
    d = pl.program_id(0)
    h = pl.program_id(1)
    r = lax.broadcasted_iota(jnp.int32, (t, t), 0)
    c = lax.broadcasted_iota(jnp.int32, (t, t), 1)
    n = jnp.maximum(r - c + d * t, 0)
    max_exact = REL_BUCKETS // 2
    log_ratio = jnp.log(jnp.maximum(n, 1).astype(F32) / max_exact) / math.log(REL_MAX_DIST / max_exact)
    large = max_exact + (log_ratio * (REL_BUCKETS - max_exact)).astype(jnp.int32)
    large = jnp.minimum(large, REL_BUCKETS - 1)
    bucket = jnp.where(n < max_exact, n, large)
    out = jnp.zeros((t, t), F32)
    for b in range(REL_BUCKETS):
        out = jnp.where(bucket == b, rb_ref[b, h], out)
    o_ref[...] = out * LOG2E


def _bias_tiles(rel_bias, t):
    assert t >= REL_MAX_DIST
    return pl.pallas_call(
        functools.partial(_bias_kernel, t=t),
        grid=(3, DSA_HEADS),
        in_specs=[pl.BlockSpec(memory_space=pltpu.SMEM)],
        out_specs=pl.BlockSpec((None, None, t, t), lambda d, h: (d, h, 0, 0)),
        out_shape=jax.ShapeDtypeStruct((3, DSA_HEADS, t, t), F32),
        compiler_params=_params(("parallel", "parallel")),
        name="rel_bias_tiles",
    )(rel_bias.astype(F32))


def _attn_kernel(qi_ref, kj_ref, q_ref, k_ref, v_ref, mask_ref, bias_ref, o_ref,
                 m_ref, acc_ref, vaug_ref, madd_ref, *, tq, tk):
    pair = pl.program_id(1)
    i = qi_ref[pair]
    j = kj_ref[pair]
    hd = DSA_HEAD_DIM
    bt = BIAS_T
    first = j == 0
    last = j == (i * tq + tq - 1) // tk

    @pl.when(pair == 0)
    def _():
        vaug_ref[...] = jnp.ones_like(vaug_ref)

    @pl.when(first)
    def _():
        m_ref[...] = jnp.full_like(m_ref, NEG_BIG)
        acc_ref[...] = jnp.zeros_like(acc_ref)

    def stage():
        madd_ref[...] = jnp.where(mask_ref[...].astype(jnp.int32) != 0, 0.0, NEG_BIG)
        for h in range(DSA_HEADS):
            vaug_ref[:, 2 * h * hd:(2 * h + 1) * hd] = v_ref[:, h * hd:(h + 1) * hd]

    lead = (i * tq - j * tk) // bt
    kinds = [[jnp.clip(lead + a - c, 0, 2) for c in range(tk // bt)] for a in range(tq // bt)]
    far = lead - (tk // bt - 1) >= 2

    def head_step(h, far_tile):
        sl = slice(h * hd, (h + 1) * hd)
        s = lax.dot_general(q_ref[:, sl], k_ref[:, sl], (((1,), (1,)), ((), ())),
                            preferred_element_type=F32)
        cols = []
        for c in range(tk // LANES):
            cs = slice(c * LANES, (c + 1) * LANES)
            blk, off = divmod(c * LANES, bt)
            t = s[:, cs] + madd_ref[:, cs]
            if not far_tile:
                t = t + jnp.concatenate([bias_ref[kinds[a][blk], h, :, off:off + LANES]
                                         for a in range(tq // bt)], axis=0)
            cols.append(t)
        mx = jnp.max(functools.reduce(jnp.maximum, cols), axis=1, keepdims=True)
        m_prev = m_ref[h]
        if far_tile:
            cb = jnp.concatenate([bias_ref[2, h, :, 0:LANES]] * (tq // bt), axis=0)
            m_new = jnp.maximum(m_prev, mx + cb)
            shift = m_new - cb
        else:
            m_new = jnp.maximum(m_prev, mx)
            shift = m_new
        alpha = jnp.exp2(m_prev - m_new)
        p = jnp.concatenate([jnp.exp2(t - shift) for t in cols], axis=1).astype(BF16)
        pv = jnp.dot(p, vaug_ref[:, 2 * h * hd:(2 * h + 2) * hd], preferred_element_type=F32)
        for half in range(2):
            a = slice((2 * h + half) * hd, (2 * h + half + 1) * hd)
            acc_ref[:, a] = alpha * acc_ref[:, a] + pv[:, half * hd:(half + 1) * hd]
        m_ref[h] = m_new

    @pl.when(far)
    def _():
        stage()
        for h in range(DSA_HEADS):
            head_step(h, True)

    @pl.when(jnp.logical_not(far))
    def _():
        stage()
        for h in range(DSA_HEADS):
            head_step(h, False)

    @pl.when(last)
    def _():
        for h in range(DSA_HEADS):
            num = acc_ref[:, 2 * h * hd:(2 * h + 1) * hd]
            den = acc_ref[:, (2 * h + 1) * hd:(2 * h + 2) * hd]
            o_ref[:, h * hd:(h + 1) * hd] = (num / den).astype(o_ref.dtype)


def _attention(p2, mask, bias, q_col, k_col, v_col, batch, seq):
    m = p2.shape[0]
    width = DSA_HEADS * DSA_HEAD_DIM
    tq = _pick(seq, (512, 256))
    tk = _pick(seq, (512, 256))
    assert tq % BIAS_T == 0 and tk % BIAS_T == 0 and DSA_HEAD_DIM == LANES
    nq, nk = seq // tq, seq // tk
    assert q_col % width == 0 and k_col % width == 0 and v_col % width == 0
    qb, kb, vb = q_col // width, k_col // width, v_col // width
    pairs = [(i, j) for i in range(nq) for j in range((i * tq + tq - 1) // tk + 1)]
    qi = jnp.asarray([p[0] for p in pairs], jnp.int32)
    kj = jnp.asarray([p[1] for p in pairs], jnp.int32)
    grid_spec = pltpu.PrefetchScalarGridSpec(
        num_scalar_prefetch=2,
        grid=(batch, len(pairs)),
        in_specs=[pl.BlockSpec((tq, width), lambda b, p, qi, kj: (b * nq + qi[p], qb)),
                  pl.BlockSpec((tk, width), lambda b, p, qi, kj: (b * nk + kj[p], kb)),
                  pl.BlockSpec((tk, width), lambda b, p, qi, kj: (b * nk + kj[p], vb)),
                  pl.BlockSpec((None, tq, tk), lambda b, p, qi, kj: (b, qi[p], kj[p])),
                  pl.BlockSpec((3, DSA_HEADS, BIAS_T, BIAS_T), lambda b, p, qi, kj: (0, 0, 0, 0))],
        out_specs=pl.BlockSpec((tq, width), lambda b, p, qi, kj: (b * nq + qi[p], 0)),
        scratch_shapes=[pltpu.VMEM((DSA_HEADS, tq, LANES), F32),
                        pltpu.VMEM((tq, 2 * width), F32),
                        pltpu.VMEM((tk, 2 * width), BF16),
                        pltpu.VMEM((tq, tk), F32)])
    return pl.pallas_call(
        functools.partial(_attn_kernel, tq=tq, tk=tk),
        grid_spec=grid_spec,
        out_shape=jax.ShapeDtypeStruct((m, width), BF16),
        compiler_params=_params(("parallel", "arbitrary")),
        name="sparse_attn",
    )(qi, kj, p2, p2, p2, mask, bias)


def _cast_kernel(x_ref, o_ref):
    o_ref[...] = x_ref[...].astype(o_ref.dtype)


def _to_bf16(w, layer, name):
    nl, r, c = w.shape
    rb = next(b for b in (1024, 512, 256, 128, 64, 32, 16) if r % b == 0 and b * c * 4 <= 8 * 2 ** 20)
    nb = r // rb
    layers = nl if layer is None else 1
    src = (lambda l, i: (l, i, 0)) if layer is None else (lambda l, i: (layer, i, 0))
    return pl.pallas_call(
        _cast_kernel,
        grid=(layers, nb),
        in_specs=[pl.BlockSpec((None, rb, c), src)],
        out_specs=pl.BlockSpec((rb, c), lambda l, i: (l * nb + i, 0)),
        out_shape=jax.ShapeDtypeStruct((layers * r, c), BF16),
        compiler_params=_params(("parallel", "parallel")),
        name=name,
    )(w)


def _regroup_kernel(w_ref, og_ref, os_ref, or_ref, *, offs, q_scale):
    x = w_ref[...]
    part = lambda i: x[:, offs[i]:offs[i + 1]]
    qkv_a, a_in, b_in, z, q_b, k_b, v_b, q_i, k_i, w_i, gate_a, gate_b = (part(i) for i in range(12))
    fill = jnp.zeros((x.shape[0], LANES - 2 * GDN_HEADS - IDX_HEADS - IDX_HEAD_DIM), x.dtype)
    og_ref[...] = jnp.concatenate([qkv_a, z], axis=1).astype(BF16)
    os_ref[...] = jnp.concatenate([a_in, b_in, w_i, fill, k_i], axis=1).astype(BF16)
    or_ref[...] = jnp.concatenate([q_b * q_scale, k_b, v_b, q_i, gate_a, gate_b], axis=1).astype(BF16)


def _split_w_in(w, layer, d_model):
    gw = GDN_HEADS * GDN_HEAD_DIM
    dw = DSA_HEADS * DSA_HEAD_DIM
    iw = IDX_HEADS * IDX_HEAD_DIM
    sizes = (3 * gw, GDN_HEADS, GDN_HEADS, gw, dw, dw, dw, iw, IDX_HEAD_DIM, IDX_HEADS, d_model, d_model)
    offs = [0]
    for s in sizes:
        offs.append(offs[-1] + s)
    _, r, d_in = w.shape
    assert offs[-1] == d_in
    rb = _pick(r, (128, 64, 32, 16))
    n_gdn, n_rest = 4 * gw, 3 * dw + iw + 2 * d_model
    w_gdn, w_small, w_rest = pl.pallas_call(
        functools.partial(_regroup_kernel, offs=tuple(offs), q_scale=DSA_HEAD_DIM ** -0.5 * LOG2E),
        grid=(r // rb,),
        in_specs=[pl.BlockSpec((None, rb, d_in), lambda i: (layer, i, 0))],
        out_specs=[pl.BlockSpec((rb, n_gdn), lambda i: (i, 0)),
                   pl.BlockSpec((rb, LANES), lambda i: (i, 0)),
                   pl.BlockSpec((rb, n_rest), lambda i: (i, 0))],
        out_shape=[jax.ShapeDtypeStruct((r, n_gdn), BF16),
                   jax.ShapeDtypeStruct((r, LANES), BF16),
                   jax.ShapeDtypeStruct((r, n_rest), BF16)],
        compiler_params=_params(("parallel",)),
        name="regroup_w_in",
    )(w)
    cols = dict(q=0, k=dw, v=2 * dw, qi=3 * dw, ga=3 * dw + iw, gb=3 * dw + iw + d_model,
                w_lane=2 * GDN_HEADS, ki_lane=LANES - IDX_HEAD_DIM)
    return w_gdn, w_small, w_rest, cols


@jax.jit
def _forward(x, rel_bias, w_in, conv_w, a_log, dt_bias, gdn_norm_w, w_branch_a, w_branch_b, w_out,
             ln1_g, ln1_b, w_ffn_in, w_ffn_out, ln2_g, ln2_b):
    batch, seq, d_model = x.shape
    depth = w_in.shape[0]
    m = batch * seq
    alpha = (2 * depth) ** 0.25
    bias = _bias_tiles(rel_bias, BIAS_T)
    xf = x.reshape(m, d_model).astype(F32)
    xb = _to_bf16(x, None, "cast_x")
    for l in range(depth):
        w_gdn, w_small, w_rest, cols = _split_w_in(w_in, l, d_model)
        p1 = _matmul(xb, w_gdn, BF16, "proj_gdn")
        ps = _matmul(xb, w_small, F32, "proj_small")
        p2 = _matmul(xb, w_rest, BF16, "proj_rest")
        o_a = _gdn(p1, ps, conv_w[l], a_log[l], dt_bias[l], gdn_norm_w[l], batch, seq)
        ki = ps[:, cols["ki_lane"]:].astype(BF16).reshape(batch, seq, IDX_HEAD_DIM)
        mask = _indexer(p2, ps, ki, cols["qi"], cols["w_lane"], batch, seq)
        o_b = _attention(p2, mask, bias, cols["q"], cols["k"], cols["v"], batch, seq)
        merged = _merge(o_a, o_b, _to_bf16(w_branch_a, l, "cast_w_a"), _to_bf16(w_branch_b, l, "cast_w_b"), p2,
                        cols["ga"], cols["gb"], d_model)
        xf, xb = _mm_res_ln(merged, _to_bf16(w_out, l, "cast_w_out"), xf, ln1_g[l], ln1_b[l], alpha,
                            "out_proj_ln")
        act = _ffn_in(xb, _to_bf16(w_ffn_in, l, "cast_w_ffn_in"))
        xf, xb = _mm_res_ln(act, _to_bf16(w_ffn_out, l, "cast_w_ffn_out"), xf, ln2_g[l], ln2_b[l], alpha,
                            "ffn_out_ln")
    return xf.reshape(batch, seq, d_model).astype(x.dtype)


def kernel(x, rel_bias, w_in, conv_w, a_log, dt_bias, gdn_norm_w, w_branch_a, w_branch_b, w_out, ln1_g, ln1_b,
           w_ffn_in, w_ffn_out, ln2_g, ln2_b):
    return _forward(x, rel_bias, w_in, conv_w, a_log, dt_bias, gdn_norm_w, w_branch_a, w_branch_b, w_out,
                    ln1_g, ln1_b, w_ffn_in, w_ffn_out, ln2_g, ln2_b)
```

```python
import functools
import math

import jax
import jax.numpy as jnp
from jax import lax
from jax.experimental import pallas as pl
from jax.experimental.pallas import tpu as pltpu

F32 = jnp.float32
BF16 = jnp.bfloat16

GDN_HEADS = 8
GDN_HEAD_DIM = 128
CONV_WIDTH = 4
CHUNK = 64
DSA_HEADS = 8
DSA_HEAD_DIM = 128
IDX_HEADS = 8
IDX_HEAD_DIM = 64
TOPK_MAX = 256
REL_BUCKETS = 32
REL_MAX_DIST = 128
LN_EPS = 1e-5
RMS_EPS = 1e-6

LANES = 128
VMEM_LIMIT = 56 * 1024 * 1024

NEG_BIG = -1e30
INT_MIN = -(2 ** 31)
LOG2E = math.log2(math.e)
BIAS_T = 256
EARLY_EXIT_PASS = 23
GDN_HEADS_PER_STEP = 4


def _params(sem):
    return pltpu.CompilerParams(dimension_semantics=sem, vmem_limit_bytes=VMEM_LIMIT)


def _pick(n, prefs):
    for p in prefs:
        if n % p == 0:
            return p
    return n


def _mm_kernel(x_ref, w_ref, o_ref):
    o_ref[...] = jnp.dot(x_ref[...], w_ref[...], preferred_element_type=F32).astype(o_ref.dtype)


def _matmul(x, w, out_dtype, name):
    m, k = x.shape
    n = w.shape[1]
    tm = _pick(m, (1024, 512, 256, 128))
    tn = _pick(n, (512, 256, 128))
    return pl.pallas_call(
        _mm_kernel,
        grid=(m // tm, n // tn),
        in_specs=[pl.BlockSpec((tm, k), lambda i, j: (i, 0)),
                  pl.BlockSpec((k, tn), lambda i, j: (0, j))],
        out_specs=pl.BlockSpec((tm, tn), lambda i, j: (i, j)),
        out_shape=jax.ShapeDtypeStruct((m, n), out_dtype),
        compiler_params=_params(("parallel", "parallel")),
        name=name,
    )(x, w)


def _ffn_in_kernel(x_ref, wg_ref, wu_ref, o_ref):
    x = x_ref[...]
    g = jnp.dot(x, wg_ref[...], preferred_element_type=F32)
    u = jnp.dot(x, wu_ref[...], preferred_element_type=F32)
    o_ref[...] = (g * jax.nn.sigmoid(g) * u).astype(o_ref.dtype)


def _ffn_in(x, w):
    m, k = x.shape
    f = w.shape[1] // 2
    tm = _pick(m, (1024, 512, 256, 128))
    tn = _pick(f, (512, 256, 128))
    nb = f // tn
    return pl.pallas_call(
        _ffn_in_kernel,
        grid=(m // tm, nb),
        in_specs=[pl.BlockSpec((tm, k), lambda i, j: (i, 0)),
                  pl.BlockSpec((k, tn), lambda i, j: (0, j)),
                  pl.BlockSpec((k, tn), lambda i, j: (0, j + nb))],
        out_specs=pl.BlockSpec((tm, tn), lambda i, j: (i, j)),
        out_shape=jax.ShapeDtypeStruct((m, f), BF16),
        compiler_params=_params(("parallel", "parallel")),
        name="ffn_in",
    )(x, w, w)


def _mm_res_ln_kernel(a_ref, w_ref, x_ref, g_ref, b_ref, of_ref, ob_ref, *acc, nk, alpha):
    def finish(prod):
        y = alpha * x_ref[...] + prod
        mu = jnp.mean(y, axis=-1, keepdims=True)
        yc = y - mu
        var = jnp.mean(yc * yc, axis=-1, keepdims=True)
        out = yc * lax.rsqrt(var + LN_EPS) * g_ref[...] + b_ref[...]
        of_ref[...] = out
        ob_ref[...] = out.astype(BF16)

    if nk == 1:
        finish(jnp.dot(a_ref[...], w_ref[...], preferred_element_type=F32))
        return
    acc_ref, = acc
    k = pl.program_id(1)

    @pl.when(k == 0)
    def _():
        acc_ref[...] = jnp.zeros_like(acc_ref)

    acc_ref[...] += jnp.dot(a_ref[...], w_ref[...], preferred_element_type=F32)

    @pl.when(k == nk - 1)
    def _():
        finish(acc_ref[...])


def _mm_res_ln(a, w, x, g, b, alpha, name):
    m, kdim = a.shape
    n = w.shape[1]
    tm = _pick(m, (512, 256, 128))
    tk = kdim if kdim * n * 2 <= 8 * 2 ** 20 else _pick(kdim, (1408, 1024, 512, 256, 128))
    nk = kdim // tk
    return pl.pallas_call(
        functools.partial(_mm_res_ln_kernel, nk=nk, alpha=alpha),
        grid=(m // tm, nk),
        in_specs=[pl.BlockSpec((tm, tk), lambda i, k: (i, k)),
                  pl.BlockSpec((tk, n), lambda i, k: (k, 0)),
                  pl.BlockSpec((tm, n), lambda i, k: (i, 0)),
                  pl.BlockSpec((1, n), lambda i, k: (0, 0)),
                  pl.BlockSpec((1, n), lambda i, k: (0, 0))],
        out_specs=[pl.BlockSpec((tm, n), lambda i, k: (i, 0)),
                   pl.BlockSpec((tm, n), lambda i, k: (i, 0))],
        out_shape=[jax.ShapeDtypeStruct((m, n), F32), jax.ShapeDtypeStruct((m, n), BF16)],
        scratch_shapes=[pltpu.VMEM((tm, n), F32)] if nk > 1 else [],
        compiler_params=_params(("parallel", "arbitrary")),
        name=name,
    )(a, w, x, g.reshape(1, n), b.reshape(1, n))


def _merge_kernel(oa_ref, ob_ref, wa_ref, wb_ref, ga_ref, gb_ref, o_ref):
    a = jnp.dot(oa_ref[...], wa_ref[...], preferred_element_type=F32)
    b = jnp.dot(ob_ref[...], wb_ref[...], preferred_element_type=F32)
    ga = jax.nn.sigmoid(ga_ref[...].astype(F32))
    gb = jax.nn.sigmoid(gb_ref[...].astype(F32))
    o_ref[...] = (ga * a + gb * b).astype(o_ref.dtype)


def _merge(o_a, o_b, w_a, w_b, p2, ga_col, gb_col, d_model):
    m, ka = o_a.shape
    kb = o_b.shape[1]
    tm = _pick(m, (1024, 512, 256, 128))
    tn = 512
    assert d_model % tn == 0 and ga_col % tn == 0 and gb_col % tn == 0
    ga_blk, gb_blk = ga_col // tn, gb_col // tn
    return pl.pallas_call(
        _merge_kernel,
        grid=(m // tm, d_model // tn),
        in_specs=[pl.BlockSpec((tm, ka), lambda i, j: (i, 0)),
                  pl.BlockSpec((tm, kb), lambda i, j: (i, 0)),
                  pl.BlockSpec((ka, tn), lambda i, j: (0, j)),
                  pl.BlockSpec((kb, tn), lambda i, j: (0, j)),
                  pl.BlockSpec((tm, tn), lambda i, j: (i, ga_blk + j)),
                  pl.BlockSpec((tm, tn), lambda i, j: (i, gb_blk + j))],
        out_specs=pl.BlockSpec((tm, tn), lambda i, j: (i, j)),
        out_shape=jax.ShapeDtypeStruct((m, d_model), BF16),
        compiler_params=_params(("parallel", "parallel")),
        name="merge",
    )(o_a, o_b, w_a, w_b, p2, p2)


def _softplus(x):
    return jnp.maximum(x, 0.0) + jnp.log1p(jnp.exp(-jnp.abs(x)))


def _gdn_kernel(q_ref, k_ref, v_ref, z_ref, sm_ref, cwq_ref, cwk_ref, cwv_ref, alog_ref, dtb_ref, nw_ref,
                o_ref, ext_ref, halo_ref, state_ref, u_ref, w_ref, qd_ref, kdt_ref, in_ref, gt_ref, *, nc, hp):
    h0 = pl.program_id(1) * hp
    tb = pl.program_id(2)
    t = nc * CHUNK
    dk = GDN_HEAD_DIM

    @pl.when(tb == 0)
    def _():
        halo_ref[...] = jnp.zeros_like(halo_ref)
        state_ref[...] = jnp.zeros_like(state_ref)

    def conv_silu(x_ref, cw_ref, slot):
        x = x_ref[...].astype(F32)
        ext_ref[0:8, :] = halo_ref[slot]
        ext_ref[8:8 + t, :] = x
        halo_ref[slot] = x[t - 8:t, :]
        cw = cw_ref[...]
        y = cw[CONV_WIDTH - 1:CONV_WIDTH, :] * x
        for j in range(CONV_WIDTH - 1):
            y = y + cw[j:j + 1, :] * ext_ref[pl.ds(8 - (CONV_WIDTH - 1) + j, t), :]
        return y * jax.nn.sigmoid(y)

    q = conv_silu(q_ref, cwq_ref, 0)
    k = conv_silu(k_ref, cwk_ref, 1)
    v = conv_silu(v_ref, cwv_ref, 2)

    sm = sm_ref[...]
    lane = lax.broadcasted_iota(jnp.int32, sm.shape, 1)
    g_all = -jnp.exp(alog_ref[...]) * _softplus(sm + dtb_ref[...])
    beta_all = jax.nn.sigmoid(sm)

    c3 = lambda a: a.reshape(nc, CHUNK, a.shape[-1])
    ri = lax.broadcasted_iota(jnp.int32, (CHUNK, CHUNK), 0)
    ci = lax.broadcasted_iota(jnp.int32, (CHUNK, CHUNK), 1)
    causal = (ci <= ri)[None]
    strict = (ci < ri)[None]
    eye = (ci == ri).astype(F32)[None]
    upper = (ri <= ci).astype(F32)[None]
    bmm = functools.partial(jnp.einsum, preferred_element_type=F32)

    for e in range(hp):
        hs = slice(e * dk, (e + 1) * dk)
        g_col = jnp.sum(jnp.where(lane == h0 + e, g_all, 0.0), axis=1, keepdims=True)
        beta = jnp.sum(jnp.where(lane == GDN_HEADS + h0 + e, beta_all, 0.0), axis=1, keepdims=True)
        qe, ke, ve = q[:, hs], k[:, hs], v[:, hs]
        qn = qe * lax.rsqrt(jnp.sum(qe * qe, axis=1, keepdims=True) + RMS_EPS) * (dk ** -0.5)
        kn = ke * lax.rsqrt(jnp.sum(ke * ke, axis=1, keepdims=True) + RMS_EPS)
        kb = kn * beta
        vb = ve * beta

        g3 = c3(g_col)
        gc_row = jnp.sum(g3 * upper, axis=1, keepdims=True)
        gc_col = jnp.sum(eye * gc_row, axis=2, keepdims=True)
        gc_last = gc_row[:, :, CHUNK - 1:CHUNK]
        decay = jnp.exp(jnp.where(causal, gc_col - gc_row, NEG_BIG))
        e_col = jnp.exp(gc_col)

        qn3, kn3, kb3, vb3 = c3(qn), c3(kn), c3(kb), c3(vb)
        kn3b = kn3.astype(BF16)
        kk = bmm("cid,cjd->cij", kb3.astype(BF16), kn3b)
        mneg = jnp.where(strict, -(kk * decay), 0.0)
        inv = eye + mneg
        pw = mneg.astype(BF16)
        for _ in range(int(math.log2(CHUNK)) - 1):
            pw = bmm("cij,cjk->cik", pw, pw).astype(BF16)
            inv = inv + bmm("cij,cjk->cik", inv.astype(BF16), pw)
        invb = inv.astype(BF16)
        u_ref[e] = bmm("cij,cjd->cid", invb, vb3.astype(BF16))
        w_ref[e] = bmm("cij,cjd->cid", invb, (kb3 * e_col).astype(BF16))
        in_ref[e] = bmm("cid,cjd->cij", qn3.astype(BF16), kn3b) * decay
        qd_ref[e] = qn3 * e_col
        kdt_ref[e] = jnp.swapaxes(kn3 * jnp.exp(gc_last - gc_col), 1, 2)
        gt_ref[e] = jnp.exp(gc_last)

    nw = nw_ref[...]
    for c in range(nc):
        rows = slice(c * CHUNK, (c + 1) * CHUNK)
        for e in range(hp):
            hs = slice(e * dk, (e + 1) * dk)
            s = state_ref[e]
            sb = s.astype(BF16)
            v_new = u_ref[e, c] - jnp.dot(w_ref[e, c].astype(BF16), sb, preferred_element_type=F32)
            vnb = v_new.astype(BF16)
            o = (jnp.dot(qd_ref[e, c].astype(BF16), sb, preferred_element_type=F32)
                 + jnp.dot(in_ref[e, c].astype(BF16), vnb, preferred_element_type=F32))
            state_ref[e] = s * gt_ref[e, c] + jnp.dot(kdt_ref[e, c].astype(BF16), vnb,
                                                      preferred_element_type=F32)
            zc = z_ref[rows, hs].astype(F32)
            o = o * lax.rsqrt(jnp.mean(o * o, axis=1, keepdims=True) + RMS_EPS) * nw * (zc * jax.nn.sigmoid(zc))
            o_ref[rows, hs] = o.astype(o_ref.dtype)


def _gdn(p1, ps, conv_w, a_log, dt_bias, norm_w, batch, seq):
    m = p1.shape[0]
    hd = GDN_HEAD_DIM
    nh = GDN_HEADS
    t = _pick(seq, (512, 256, 128, 64))
    nc = t // CHUNK
    nt = seq // t
    hp = GDN_HEADS_PER_STEP
    ng = nh // hp
    wd = hp * hd
    pad = lambda a: jnp.pad(a.astype(F32), (0, LANES - a.shape[0])).reshape(1, LANES)
    row = lambda b, h, i: b * nt + i
    col_spec = lambda off: pl.BlockSpec((t, wd), lambda b, h, i: (row(b, h, i), off + h))
    cw_spec = lambda off: pl.BlockSpec((CONV_WIDTH, wd), lambda b, h, i: (0, off + h))
    vec_spec = pl.BlockSpec((1, LANES), lambda b, h, i: (0, 0))
    return pl.pallas_call(
        functools.partial(_gdn_kernel, nc=nc, hp=hp),
        grid=(batch, ng, nt),
        in_specs=[col_spec(0), col_spec(ng), col_spec(2 * ng), col_spec(3 * ng),
                  pl.BlockSpec((t, LANES), lambda b, h, i: (row(b, h, i), 0)),
                  cw_spec(0), cw_spec(ng), cw_spec(2 * ng),
                  vec_spec, vec_spec, vec_spec],
        out_specs=pl.BlockSpec((t, wd), lambda b, h, i: (row(b, h, i), h)),
        out_shape=jax.ShapeDtypeStruct((m, nh * hd), BF16),
        scratch_shapes=[pltpu.VMEM((t + 8, wd), F32),
                        pltpu.VMEM((3, 8, wd), F32),
                        pltpu.VMEM((hp, hd, hd), F32),
                        pltpu.VMEM((hp, nc, CHUNK, hd), F32),
                        pltpu.VMEM((hp, nc, CHUNK, hd), F32),
                        pltpu.VMEM((hp, nc, CHUNK, hd), F32),
                        pltpu.VMEM((hp, nc, hd, CHUNK), F32),
                        pltpu.VMEM((hp, nc, CHUNK, CHUNK), F32),
                        pltpu.VMEM((hp, nc, 1, 1), F32)],
        compiler_params=_params(("parallel", "parallel", "arbitrary")),
        name="gdn",
    )(p1, p1, p1, p1, ps, conv_w, conv_w, conv_w, pad(a_log), pad(dt_bias), norm_w.reshape(1, hd).astype(F32))


def _key_to_float(key):
    bits = jnp.where(key >= 0, key, key ^ jnp.int32(0x7FFFFFFF))
    return lax.bitcast_convert_type(bits, F32)


def _upper_half(x):
    bits = lax.bitcast_convert_type(x, jnp.int32) & jnp.int32(-65536)
    return lax.bitcast_convert_type(bits, F32).astype(BF16)


def _indexer_kernel(qi_ref, ws_ref, ki_ref, mask_ref, sc_ref, sch_ref, *, tq, tk, seq, topk, w_lane):
    i = pl.program_id(1)
    t0 = i * tq
    nt = (t0 + tq + tk - 1) // tk
    n_tiles = seq // tk
    sl = 8
    ways = 2
    ways16 = 4
    key0 = lax.broadcasted_iota(jnp.int32, (tk, tq), 0)
    qpos = t0 + lax.broadcasted_iota(jnp.int32, (tk, tq), 1)

    qi = qi_ref[...]
    ws_t = jnp.transpose(ws_ref[...])
    q_heads = [qi[:, h * IDX_HEAD_DIM:(h + 1) * IDX_HEAD_DIM] for h in range(IDX_HEADS)]
    w_rows = [ws_t[w_lane + h:w_lane + h + 1, :] for h in range(IDX_HEADS)]

    def score_tile(j, carry):
        s0 = pl.multiple_of(j * tk, tk)
        kt = ki_ref[pl.ds(s0, tk), :]
        acc = jnp.zeros((tk, tq), F32)
        for h in range(IDX_HEADS):
            r = lax.dot_general(kt, q_heads[h], (((1,), (1,)), ((), ())), preferred_element_type=F32)
            acc = acc + w_rows[h] * jnp.maximum(r, 0.0)
        sc = jnp.where(key0 + s0 <= qpos, acc, -jnp.inf)
        sc_ref[pl.ds(s0, tk), :] = sc
        sch_ref[pl.ds(s0, tk), :] = _upper_half(sc)
        return carry

    lax.fori_loop(0, nt, score_tile, 0)

    def count_upper(cf):
        pk = 2 * sl
        ch = jnp.broadcast_to(_upper_half(cf), (pk * ways16, tq))

        def body(j, acc):
            s0 = pl.multiple_of(j * tk, tk)
            for g in range(tk // (pk * ways16)):
                x = sch_ref[pl.ds(s0 + g * pk * ways16, pk * ways16), :]
                ind = jnp.where(x >= ch, jnp.bfloat16(1), jnp.bfloat16(0))
                acc = acc + ind.reshape(ways16, pk, tq)
            return acc
        assert (seq // (pk * ways16)) < 256
        acc = lax.fori_loop(0, nt, body, jnp.zeros((ways16, pk, tq), BF16))
        return jnp.sum(jnp.sum(acc.astype(F32), axis=0), axis=0, keepdims=True)

    def count(pred):
        def body(j, acc):
            s0 = pl.multiple_of(j * tk, tk)
            for g in range(tk // (sl * ways)):
                x = sc_ref[pl.ds(s0 + g * sl * ways, sl * ways), :]
                acc = acc + jnp.where(pred(x), 1.0, 0.0).reshape(ways, sl, tq)
            return acc
        acc = lax.fori_loop(0, nt, body, jnp.zeros((ways, sl, tq), F32))
        return jnp.sum(jnp.sum(acc, axis=0), axis=0, keepdims=True)

    kf = jnp.float32(topk)
    rows = lambda a: jnp.broadcast_to(a, (tk, tq))
    c_nonneg = count_upper(jnp.zeros((1, tq), F32))
    nonneg = c_nonneg >= kf
    base0 = jnp.where(nonneg, jnp.int32(0), jnp.int32(INT_MIN))
    cnt0 = jnp.where(nonneg, c_nonneg, jnp.float32(seq + 1))

    def bit_step(count_ge):
        def step(b, carry):
            base, cnt = carry
            cand = base | jnp.left_shift(jnp.int32(1), 30 - b)
            c = count_ge(_key_to_float(cand))
            ok = c >= kf
            return jnp.where(ok, cand, base), jnp.where(ok, c, cnt)
        return step

    fine_step = bit_step(lambda cf: count(lambda x: x >= cf))
    carry = lax.fori_loop(0, 15, bit_step(count_upper), (base0, cnt0))
    carry = lax.fori_loop(15, EARLY_EXIT_PASS, fine_step, carry)
    unsettled = lambda s: jnp.logical_and(s[0] < 31, jnp.max(jnp.where(s[2] != kf, 1.0, 0.0)) > 0.0)
    _, base, n_ge = lax.while_loop(unsettled, lambda s: (s[0] + 1,) + fine_step(s[0], s[1:]),
                                   (jnp.int32(EARLY_EXIT_PASS),) + carry)
    no_thr = base == jnp.int32(INT_MIN)
    take_all = rows(no_thr)
    thr_row = _key_to_float(base)
    thr = rows(thr_row)
    tie_excess = jnp.logical_and(jnp.logical_not(no_thr), n_ge > kf)
    any_excess = jnp.max(jnp.where(tie_excess, 1.0, 0.0)) > 0.0

    def write(j, sel):
        s0 = pl.multiple_of(j * tk, tk)
        keep = jnp.logical_and(key0 + s0 <= qpos, jnp.logical_or(take_all, sel))
        keep_qk = jnp.transpose(jnp.where(keep, 1.0, 0.0))
        mask_ref[:, pl.ds(s0, tk)] = jnp.where(keep_qk > 0.5, 1, 0).astype(jnp.int8)

    @pl.when(jnp.logical_not(any_excess))
    def _():
        def body(j, carry):
            s0 = pl.multiple_of(j * tk, tk)
            write(j, sc_ref[pl.ds(s0, tk), :] >= thr)
            return carry
        lax.fori_loop(0, nt, body, 0)

    @pl.when(any_excess)
    def _():
        need = rows(kf - count(lambda x: x > thr_row))
        earlier = (lax.broadcasted_iota(jnp.int32, (tk, tk), 1)
                   < lax.broadcasted_iota(jnp.int32, (tk, tk), 0)).astype(BF16)

        def body(j, seen):
            s0 = pl.multiple_of(j * tk, tk)
            x = sc_ref[pl.ds(s0, tk), :]
            eq = x == thr
            eqf = jnp.where(eq, 1.0, 0.0)
            rank = rows(seen) + jnp.dot(earlier, eqf.astype(BF16), preferred_element_type=F32)
            write(j, jnp.logical_or(x > thr, jnp.logical_and(eq, rank < need)))
            return seen + jnp.sum(eqf, axis=0, keepdims=True)
        lax.fori_loop(0, nt, body, jnp.zeros((1, tq), F32))

    def clear(j, carry):
        s0 = pl.multiple_of(j * tk, tk)
        mask_ref[:, pl.ds(s0, tk)] = jnp.zeros((tq, tk), jnp.int8)
        return carry

    lax.fori_loop(nt, n_tiles, clear, 0)


def _indexer(p2, ps, ki, qi_col, w_lane, batch, seq):
    tq = _pick(seq, (512, 256, 128))
    tk = _pick(seq, (512, 256, 128))
    nq = seq // tq
    qw = IDX_HEADS * IDX_HEAD_DIM
    assert qi_col % qw == 0
    topk = min(TOPK_MAX, seq // 4)
    return pl.pallas_call(
        functools.partial(_indexer_kernel, tq=tq, tk=tk, seq=seq, topk=topk, w_lane=w_lane),
        grid=(batch, nq),
        in_specs=[pl.BlockSpec((tq, qw), lambda b, i: (b * nq + i, qi_col // qw)),
                  pl.BlockSpec((tq, LANES), lambda b, i: (b * nq + i, 0)),
                  pl.BlockSpec((None, seq, IDX_HEAD_DIM), lambda b, i: (b, 0, 0))],
        out_specs=pl.BlockSpec((None, tq, seq), lambda b, i: (b, i, 0)),
        out_shape=jax.ShapeDtypeStruct((batch, seq, seq), jnp.int8),
        scratch_shapes=[pltpu.VMEM((seq, tq), F32),
                        pltpu.VMEM((seq, tq), BF16)],
        compiler_params=_params(("parallel", "parallel")),
        name="indexer",
    )(p2, ps, ki)


def _bias_kernel(rb_ref, o_ref, *, t):
    d = pl.program_id(0)
    h = pl.program_id(1)
    r = lax.broadcasted_iota(jnp.int32, (t, t), 0)
    c = lax.broadcasted_iota(jnp.int32, (t, t), 1)
    n = jnp.maximum(r - c + d * t, 0)
    max_exact = REL_BUCKETS // 2
    log_ratio = jnp.log(jnp.maximum(n, 1).astype(F32) / max_exact) / math.log(REL_MAX_DIST / max_exact)
    large = max_exact + (log_ratio * (REL_BUCKETS - max_exact)).astype(jnp.int32)
    large = jnp.minimum(large, REL_BUCKETS - 1)
    bucket = jnp.where(n < max_exact, n, large)
    out = jnp.zeros((t, t), F32)
    for b in range(REL_BUCKETS):
        out = jnp.where(bucket == b, rb_ref[b, h], out)
    o_ref[...] = out * LOG2E


def _bias_tiles(rel_bias, t):
    assert t >= REL_MAX_DIST
    return pl.pallas_call(
        functools.partial(_bias_kernel, t=t),
        grid=(3, DSA_HEADS),
        in_specs=[pl.BlockSpec(memory_space=pltpu.SMEM)],
        out_specs=pl.BlockSpec((None, None, t, t), lambda d, h: (d, h, 0, 0)),
        out_shape=jax.ShapeDtypeStruct((3, DSA_HEADS, t, t), F32),
        compiler_params=_params(("parallel", "parallel")),
        name="rel_bias_tiles",
    )(rel_bias.astype(F32))


def _attn_kernel(qi_ref, kj_ref, q_ref, k_ref, v_ref, mask_ref, bias_ref, o_ref,
                 m_ref, acc_ref, vaug_ref, madd_ref, *, tq, tk):
    pair = pl.program_id(1)
    i = qi_ref[pair]
    j = kj_ref[pair]
    hd = DSA_HEAD_DIM
    bt = BIAS_T
    first = j == 0
    last = j == (i * tq + tq - 1) // tk

    @pl.when(pair == 0)
    def _():
        vaug_ref[...] = jnp.ones_like(vaug_ref)

    @pl.when(first)
    def _():
        m_ref[...] = jnp.full_like(m_ref, NEG_BIG)
        acc_ref[...] = jnp.zeros_like(acc_ref)

    def stage():
        madd_ref[...] = jnp.where(mask_ref[...].astype(jnp.int32) != 0, 0.0, NEG_BIG)
        for h in range(DSA_HEADS):
            vaug_ref[:, 2 * h * hd:(2 * h + 1) * hd] = v_ref[:, h * hd:(h + 1) * hd]

    lead = (i * tq - j * tk) // bt
    kinds = [[jnp.clip(lead + a - c, 0, 2) for c in range(tk // bt)] for a in range(tq // bt)]
    far = lead - (tk // bt - 1) >= 2

    def head_step(h, far_tile):
        sl = slice(h * hd, (h + 1) * hd)
        s = lax.dot_general(q_ref[:, sl], k_ref[:, sl], (((1,), (1,)), ((), ())),
                            preferred_element_type=F32)
        cols = []
        for c in range(tk // LANES):
            cs = slice(c * LANES, (c + 1) * LANES)
            blk, off = divmod(c * LANES, bt)
            t = s[:, cs] + madd_ref[:, cs]
            if not far_tile:
                t = t + jnp.concatenate([bias_ref[kinds[a][blk], h, :, off:off + LANES]
                                         for a in range(tq // bt)], axis=0)
            cols.append(t)
        mx = jnp.max(functools.reduce(jnp.maximum, cols), axis=1, keepdims=True)
        m_prev = m_ref[h]
        if far_tile:
            cb = jnp.concatenate([bias_ref[2, h, :, 0:LANES]] * (tq // bt), axis=0)
            m_new = jnp.maximum(m_prev, mx + cb)
            shift = m_new - cb
        else:
            m_new = jnp.maximum(m_prev, mx)
            shift = m_new
        alpha = jnp.exp2(m_prev - m_new)
        p = jnp.concatenate([jnp.exp2(t - shift) for t in cols], axis=1).astype(BF16)
        pv = jnp.dot(p, vaug_ref[:, 2 * h * hd:(2 * h + 2) * hd], preferred_element_type=F32)
        for half in range(2):
            a = slice((2 * h + half) * hd, (2 * h + half + 1) * hd)
            acc_ref[:, a] = alpha * acc_ref[:, a] + pv[:, half * hd:(half + 1) * hd]
        m_ref[h] = m_new

    @pl.when(far)
    def _():
        stage()
        for h in range(DSA_HEADS):
            head_step(h, True)

    @pl.when(jnp.logical_not(far))
    def _():
        stage()
        for h in range(DSA_HEADS):
            head_step(h, False)

    @pl.when(last)
    def _():
        for h in range(DSA_HEADS):
            num = acc_ref[:, 2 * h * hd:(2 * h + 1) * hd]
            den = acc_ref[:, (2 * h + 1) * hd:(2 * h + 2) * hd]
            o_ref[:, h * hd:(h + 1) * hd] = (num / den).astype(o_ref.dtype)


def _attention(p2, mask, bias, q_col, k_col, v_col, batch, seq):
    m = p2.shape[0]
    width = DSA_HEADS * DSA_HEAD_DIM
    tq = _pick(seq, (512, 256))
    tk = _pick(seq, (512, 256))
    assert tq % BIAS_T == 0 and tk % BIAS_T == 0 and DSA_HEAD_DIM == LANES
    nq, nk = seq // tq, seq // tk
    assert q_col % width == 0 and k_col % width == 0 and v_col % width == 0
    qb, kb, vb = q_col // width, k_col // width, v_col // width
    pairs = [(i, j) for i in range(nq) for j in range((i * tq + tq - 1) // tk + 1)]
    qi = jnp.asarray([p[0] for p in pairs], jnp.int32)
    kj = jnp.asarray([p[1] for p in pairs], jnp.int32)
    grid_spec = pltpu.PrefetchScalarGridSpec(
        num_scalar_prefetch=2,
        grid=(batch, len(pairs)),
        in_specs=[pl.BlockSpec((tq, width), lambda b, p, qi, kj: (b * nq + qi[p], qb)),
                  pl.BlockSpec((tk, width), lambda b, p, qi, kj: (b * nk + kj[p], kb)),
                  pl.BlockSpec((tk, width), lambda b, p, qi, kj: (b * nk + kj[p], vb)),
                  pl.BlockSpec((None, tq, tk), lambda b, p, qi, kj: (b, qi[p], kj[p])),
                  pl.BlockSpec((3, DSA_HEADS, BIAS_T, BIAS_T), lambda b, p, qi, kj: (0, 0, 0, 0))],
        out_specs=pl.BlockSpec((tq, width), lambda b, p, qi, kj: (b * nq + qi[p], 0)),
        scratch_shapes=[pltpu.VMEM((DSA_HEADS, tq, LANES), F32),
                        pltpu.VMEM((tq, 2 * width), F32),
                        pltpu.VMEM((tk, 2 * width), BF16),
                        pltpu.VMEM((tq, tk), F32)])
    return pl.pallas_call(
        functools.partial(_attn_kernel, tq=tq, tk=tk),
        grid_spec=grid_spec,
        out_shape=jax.ShapeDtypeStruct((m, width), BF16),
        compiler_params=_params(("parallel", "arbitrary")),
        name="sparse_attn",
    )(qi, kj, p2, p2, p2, mask, bias)


def _cast_kernel(x_ref, o_ref):
    o_ref[...] = x_ref[...].astype(o_ref.dtype)


def _to_bf16(w, layer, name):
    nl, r, c = w.shape
    rb = next(b for b in (1024, 512, 256, 128, 64, 32, 16) if r % b == 0 and b * c * 4 <= 8 * 2 ** 20)
    nb = r // rb
    layers = nl if layer is None else 1
    src = (lambda l, i: (l, i, 0)) if layer is None else (lambda l, i: (layer, i, 0))
    return pl.pallas_call(
        _cast_kernel,
        grid=(layers, nb),
        in_specs=[pl.BlockSpec((None, rb, c), src)],
        out_specs=pl.BlockSpec((rb, c), lambda l, i: (l * nb + i, 0)),
        out_shape=jax.ShapeDtypeStruct((layers * r, c), BF16),
        compiler_params=_params(("parallel", "parallel")),
        name=name,
    )(w)


def _regroup_kernel(w_ref, og_ref, os_ref, or_ref, *, offs, q_scale):
    x = w_ref[...]
    part = lambda i: x[:, offs[i]:offs[i + 1]]
    qkv_a, a_in, b_in, z, q_b, k_b, v_b, q_i, k_i, w_i, gate_a, gate_b = (part(i) for i in range(12))
    fill = jnp.zeros((x.shape[0], LANES - 2 * GDN_HEADS - IDX_HEADS - IDX_HEAD_DIM), x.dtype)
    og_ref[...] = jnp.concatenate([qkv_a, z], axis=1).astype(BF16)
    os_ref[...] = jnp.concatenate([a_in, b_in, w_i, fill, k_i], axis=1).astype(BF16)
    or_ref[...] = jnp.concatenate([q_b * q_scale, k_b, v_b, q_i, gate_a, gate_b], axis=1).astype(BF16)


def _split_w_in(w, layer, d_model):
    gw = GDN_HEADS * GDN_HEAD_DIM
    dw = DSA_HEADS * DSA_HEAD_DIM
    iw = IDX_HEADS * IDX_HEAD_DIM
    sizes = (3 * gw, GDN_HEADS, GDN_HEADS, gw, dw, dw, dw, iw, IDX_HEAD_DIM, IDX_HEADS, d_model, d_model)
    offs = [0]
    for s in sizes:
        offs.append(offs[-1] + s)
    _, r, d_in = w.shape
    assert offs[-1] == d_in
    rb = _pick(r, (128, 64, 32, 16))
    n_gdn, n_rest = 4 * gw, 3 * dw + iw + 2 * d_model
    w_gdn, w_small, w_rest = pl.pallas_call(
        functools.partial(_regroup_kernel, offs=tuple(offs), q_scale=DSA_HEAD_DIM ** -0.5 * LOG2E),
        grid=(r // rb,),
        in_specs=[pl.BlockSpec((None, rb, d_in), lambda i: (layer, i, 0))],
        out_specs=[pl.BlockSpec((rb, n_gdn), lambda i: (i, 0)),
                   pl.BlockSpec((rb, LANES), lambda i: (i, 0)),
                   pl.BlockSpec((rb, n_rest), lambda i: (i, 0))],
        out_shape=[jax.ShapeDtypeStruct((r, n_gdn), BF16),
                   jax.ShapeDtypeStruct((r, LANES), BF16),
                   jax.ShapeDtypeStruct((r, n_rest), BF16)],
        compiler_params=_params(("parallel",)),
        name="regroup_w_in",
    )(w)
    cols = dict(q=0, k=dw, v=2 * dw, qi=3 * dw, ga=3 * dw + iw, gb=3 * dw + iw + d_model,
                w_lane=2 * GDN_HEADS, ki_lane=LANES - IDX_HEAD_DIM)
    return w_gdn, w_small, w_rest, cols


@jax.jit
def _forward(x, rel_bias, w_in, conv_w, a_log, dt_bias, gdn_norm_w, w_branch_a, w_branch_b, w_out,
             ln1_g, ln1_b, w_ffn_in, w_ffn_out, ln2_g, ln2_b):
    batch, seq, d_model = x.shape
    depth = w_in.shape[0]
    m = batch * seq
    alpha = (2 * depth) ** 0.25
    bias = _bias_tiles(rel_bias, BIAS_T)
    xf = x.reshape(m, d_model).astype(F32)
    xb = _to_bf16(x, None, "cast_x")
    for l in range(depth):
        w_gdn, w_small, w_rest, cols = _split_w_in(w_in, l, d_model)
        p1 = _matmul(xb, w_gdn, BF16, "proj_gdn")
        ps = _matmul(xb, w_small, F32, "proj_small")
        p2 = _matmul(xb, w_rest, BF16, "proj_rest")
        o_a = _gdn(p1, ps, conv_w[l], a_log[l], dt_bias[l], gdn_norm_w[l], batch, seq)
        ki = ps[:, cols["ki_lane"]:].astype(BF16).reshape(batch, seq, IDX_HEAD_DIM)
        mask = _indexer(p2, ps, ki, cols["qi"], cols["w_lane"], batch, seq)
        o_b = _attention(p2, mask, bias, cols["q"], cols["k"], cols["v"], batch, seq)
        merged = _merge(o_a, o_b, _to_bf16(w_branch_a, l, "cast_w_a"), _to_bf16(w_branch_b, l, "cast_w_b"), p2,
                        cols["ga"], cols["gb"], d_model)
        xf, xb = _mm_res_ln(merged, _to_bf16(w_out, l, "cast_w_out"), xf, ln1_g[l], ln1_b[l], alpha,
                            "out_proj_ln")
        act = _ffn_in(xb, _to_bf16(w_ffn_in, l, "cast_w_ffn_in"))
        xf, xb = _mm_res_ln(act, _to_bf16(w_ffn_out, l, "cast_w_ffn_out"), xf, ln2_g[l], ln2_b[l], alpha,
                            "ffn_out_ln")
    return xf.reshape(batch, seq, d_model).astype(x.dtype)


def kernel(x, rel_bias, w_in, conv_w, a_log, dt_bias, gdn_norm_w, w_branch_a, w_branch_b, w_out, ln1_g, ln1_b,
           w_ffn_in, w_ffn_out, ln2_g, ln2_b):
    return _forward(x, rel_bias, w_in, conv_w, a_log, dt_bias, gdn_norm_w, w_branch_a, w_branch_b, w_out,
                    ln1_g, ln1_b, w_ffn_in, w_ffn_out, ln2_g, ln2_b)
```
